```python
import math
import jax, jax.numpy as jnp
from jax import lax
import numpy as np

D_MODEL = 2048
BATCH = 8
SEQ = 2048
DEPTH = 1

MIX_WIDTH = D_MODEL
SSM_WIDTH = MIX_WIDTH // 2
SSM_GROUP = 16
SSM_GROUPS = SSM_WIDTH // SSM_GROUP
SSM_STATE = 64
DT_MIN = 1e-3
DT_MAX = 1e-1
QK_NOPE_DIM = 128
QK_ROPE_DIM = 64
V_HEAD_DIM = 128
MLA_WIDTH = MIX_WIDTH - SSM_WIDTH
MLA_HEADS = MLA_WIDTH // V_HEAD_DIM
Q_LORA_RANK = D_MODEL // 4
KV_LORA_RANK = D_MODEL // 8
ROPE_THETA = 10000.0
Q_BLOCK = 128
IN_WIDTH = SSM_WIDTH + Q_LORA_RANK + KV_LORA_RANK + QK_ROPE_DIM
D_FF = ((8 * D_MODEL // 3 + 255) // 256) * 256
CONV_WIDTH = 3
RMS_EPS = 1e-6

kernel_name = 'hybrid_s5_mla_convffn_layer'


def _rmsnorm(x, w):
    xf = x.astype(jnp.float32)
    y = xf * lax.rsqrt(jnp.mean(xf * xf, axis=-1, keepdims=True) + RMS_EPS)
    return (y * w.astype(jnp.float32)).astype(x.dtype)


def _rope_tables(positions, dtype):
    inv_freq = ROPE_THETA ** (-jnp.arange(0, QK_ROPE_DIM, 2, dtype=jnp.float32) / QK_ROPE_DIM)
    ang = positions.astype(jnp.float32)[..., None] * inv_freq
    return jnp.cos(ang).astype(dtype), jnp.sin(ang).astype(dtype)


def _rope(x, cos, sin):
    x1, x2 = jnp.split(x, 2, axis=-1)
    return jnp.concatenate([x1 * cos - x2 * sin, x1 * sin + x2 * cos], axis=-1)


def _ssm_combine(left, right):
    ar1, ai1, br1, bi1 = left
    ar2, ai2, br2, bi2 = right
    ar = ar2 * ar1 - ai2 * ai1
    ai = ar2 * ai1 + ai2 * ar1
    br = ar2 * br1 - ai2 * bi1 + br2
    bi = ar2 * bi1 + ai2 * br1 + bi2
    return ar, ai, br, bi


def _s5_group(u, lam_re, lam_im, log_dt, b_re, b_im, c_re, c_im, d_skip, w_glu, b_glu):
    bsz, seq, _ = u.shape
    ug = u.astype(jnp.float32).reshape(bsz, seq, SSM_GROUPS, SSM_GROUP)
    lr = lam_re.astype(jnp.float32)
    li = lam_im.astype(jnp.float32)
    dt = jnp.exp(log_dt.astype(jnp.float32))[:, None]
    mag = jnp.exp(lr * dt)
    abar_re = mag * jnp.cos(li * dt)
    abar_im = mag * jnp.sin(li * dt)
    nr, ni = abar_re - 1.0, abar_im
    den = lr * lr + li * li
    zr = (nr * lr + ni * li) / den
    zi = (ni * lr - nr * li) / den
    bre = b_re.astype(jnp.float32)
    bim = b_im.astype(jnp.float32)
    bbar_re = zr[..., None] * bre - zi[..., None] * bim
    bbar_im = zr[..., None] * bim + zi[..., None] * bre
    bu_re = jnp.einsum('blgh,gph->lbgp', ug, bbar_re)
    bu_im = jnp.einsum('blgh,gph->lbgp', ug, bbar_im)
    a_re = jnp.broadcast_to(abar_re, (seq, 1, SSM_GROUPS, SSM_STATE))
    a_im = jnp.broadcast_to(abar_im, (seq, 1, SSM_GROUPS, SSM_STATE))
    _, _, s_re, s_im = lax.associative_scan(_ssm_combine, (a_re, a_im, bu_re, bu_im), axis=0)
    y = (jnp.einsum('lbgp,ghp->blgh', s_re, c_re.astype(jnp.float32))
         - jnp.einsum('lbgp,ghp->blgh', s_im, c_im.astype(jnp.float32))
         + d_skip.astype(jnp.float32).reshape(SSM_GROUPS, SSM_GROUP) * ug)
    y = jax.nn.gelu(y.reshape(bsz, seq, SSM_WIDTH)).astype(u.dtype)
    return y * jax.nn.sigmoid(y @ w_glu + b_glu)


def _mla_group(c_q, c_kv, k_pe, positions, q_norm_w, w_uq, kv_norm_w, w_ukv):
    bsz, seq, _ = c_q.shape
    q = (_rmsnorm(c_q, q_norm_w) @ w_uq).reshape(bsz, seq, MLA_HEADS, QK_NOPE_DIM + QK_ROPE_DIM)
    q_nope, q_pe = q[..., :QK_NOPE_DIM], q[..., QK_NOPE_DIM:]
    kv = (_rmsnorm(c_kv, kv_norm_w) @ w_ukv).reshape(bsz, seq, MLA_HEADS, QK_NOPE_DIM + V_HEAD_DIM)
    k_nope, v = kv[..., :QK_NOPE_DIM], kv[..., QK_NOPE_DIM:]
    cos, sin = _rope_tables(positions, q.dtype)
    q_pe = _rope(q_pe, cos[:, :, None, :], sin[:, :, None, :])
    k_pe = _rope(k_pe, cos, sin)
    scale = (QK_NOPE_DIM + QK_ROPE_DIM) ** -0.5
    neg = jnp.finfo(jnp.float32).min
    outs = []
    for blk in range(seq // Q_BLOCK):
        q0 = blk * Q_BLOCK
        kend = q0 + Q_BLOCK
        s = (jnp.einsum('bqhd,bkhd->bhqk', q_nope[:, q0:kend], k_nope[:, :kend])
             + jnp.einsum('bqhr,bkr->bhqk', q_pe[:, q0:kend], k_pe[:, :kend]))
        s = s.astype(jnp.float32) * scale
        causal = jnp.arange(kend)[None, :] <= (q0 + jnp.arange(Q_BLOCK))[:, None]
        s = jnp.where(causal, s, neg)
        p = jax.nn.softmax(s, axis=-1).astype(v.dtype)
        outs.append(jnp.einsum('bhqk,bkhd->bqhd', p, v[:, :kend]))
    o = jnp.concatenate(outs, axis=1)
    return o.reshape(bsz, seq, MLA_WIDTH)


def _conv_ffn(h, w_up, conv_w, conv_b, w_down):
    a = h @ w_up
    a = lax.conv_general_dilated(a, conv_w[:, None, :], window_strides=(1,),
                                 padding=[(CONV_WIDTH - 1, 0)],
                                 dimension_numbers=('NWC', 'WIO', 'NWC'),
                                 feature_group_count=2 * D_FF) + conv_b
    gate, val = jnp.split(a, 2, axis=-1)
    return (jax.nn.silu(gate) * val) @ w_down


def setup_inputs(seed: int = 0) -> dict:
    key = jax.random.key(seed)
    ks = jax.random.split(key, 32)
    f32 = jnp.float32

    def nrm(k, shape, scale):
        return jax.random.normal(k, (DEPTH,) + shape, f32) * scale

    def gain(k, n):
        return 1.0 + 0.02 * jax.random.normal(k, (DEPTH, n), f32)

    x = jax.random.normal(ks[0], (BATCH, SEQ, D_MODEL), f32)
    offs = jax.random.randint(ks[1], (BATCH, 1), 0, 1024, dtype=jnp.int32)
    positions = offs + jnp.arange(SEQ, dtype=jnp.int32)[None, :]
    lam_re = -0.5 + 0.01 * jax.random.normal(ks[4], (DEPTH, SSM_GROUPS, SSM_STATE), f32)
    lam_im = (math.pi * jnp.arange(SSM_STATE, dtype=f32))[None, None, :] + 0.01 * jax.random.normal(ks[5], (DEPTH, SSM_GROUPS, SSM_STATE), f32)
    log_dt = jax.random.uniform(ks[6], (DEPTH, SSM_GROUPS), f32, math.log(DT_MIN), math.log(DT_MAX))
    return {
        'x': x,
        'positions': positions,
        'attn_norm_w': gain(ks[2], D_MODEL),
        'w_in': nrm(ks[3], (D_MODEL, IN_WIDTH), D_MODEL ** -0.5),
        'ssm_lambda_re': lam_re,
        'ssm_lambda_im': lam_im,
        'ssm_log_dt': log_dt,
        'ssm_b_re': nrm(ks[7], (SSM_GROUPS, SSM_STATE, SSM_GROUP), (2 * SSM_GROUP) ** -0.5),
        'ssm_b_im': nrm(ks[8], (SSM_GROUPS, SSM_STATE, SSM_GROUP), (2 * SSM_GROUP) ** -0.5),
        'ssm_c_re': nrm(ks[9], (SSM_GROUPS, SSM_GROUP, SSM_STATE), (2 * SSM_STATE) ** -0.5),
        'ssm_c_im': nrm(ks[10], (SSM_GROUPS, SSM_GROUP, SSM_STATE), (2 * SSM_STATE) ** -0.5),
        'ssm_d': nrm(ks[11], (SSM_WIDTH,), 1.0),
        'ssm_w_glu': nrm(ks[12], (SSM_WIDTH, SSM_WIDTH), SSM_WIDTH ** -0.5),
        'ssm_b_glu': nrm(ks[13], (SSM_WIDTH,), 0.01),
        'mla_q_norm_w': gain(ks[14], Q_LORA_RANK),
        'mla_w_uq': nrm(ks[15], (Q_LORA_RANK, MLA_HEADS * (QK_NOPE_DIM + QK_ROPE_DIM)), Q_LORA_RANK ** -0.5),
        'mla_kv_norm_w': gain(ks[16], KV_LORA_RANK),
        'mla_w_ukv': nrm(ks[17], (KV_LORA_RANK, MLA_HEADS * (QK_NOPE_DIM + V_HEAD_DIM)), KV_LORA_RANK ** -0.5),
        'ssm_out_norm_w': gain(ks[18], SSM_WIDTH),
        'mla_out_norm_w': gain(ks[19], MLA_WIDTH),
        'w_out': nrm(ks[20], (MIX_WIDTH, D_MODEL), MIX_WIDTH ** -0.5),
        'ffn_norm_w': gain(ks[21], D_MODEL),
        'ffn_w_up': nrm(ks[22], (D_MODEL, 2 * D_FF), D_MODEL ** -0.5),
        'ffn_conv_w': nrm(ks[23], (CONV_WIDTH, 2 * D_FF), CONV_WIDTH ** -0.5),
        'ffn_conv_b': nrm(ks[24], (2 * D_FF,), 0.01),
        'ffn_w_down': nrm(ks[25], (D_FF, D_MODEL), D_FF ** -0.5),
        'final_norm_w': 1.0 + 0.02 * jax.random.normal(ks[26], (D_MODEL,), f32),
    }


def reference(x, positions, attn_norm_w, w_in, ssm_lambda_re, ssm_lambda_im, ssm_log_dt,
              ssm_b_re, ssm_b_im, ssm_c_re, ssm_c_im, ssm_d, ssm_w_glu, ssm_b_glu,
              mla_q_norm_w, mla_w_uq, mla_kv_norm_w, mla_w_ukv, ssm_out_norm_w,
              mla_out_norm_w, w_out, ffn_norm_w, ffn_w_up, ffn_conv_w, ffn_conv_b,
              ffn_w_down, final_norm_w):
    split_at = [SSM_WIDTH, SSM_WIDTH + Q_LORA_RANK, SSM_WIDTH + Q_LORA_RANK + KV_LORA_RANK]
    h = x
    for l in range(DEPTH):
        hn = _rmsnorm(h, attn_norm_w[l])
        proj = hn @ w_in[l]
        u, c_q, c_kv, k_pe = jnp.split(proj, split_at, axis=-1)
        y_ssm = _s5_group(u, ssm_lambda_re[l], ssm_lambda_im[l], ssm_log_dt[l],
                          ssm_b_re[l], ssm_b_im[l], ssm_c_re[l], ssm_c_im[l],
                          ssm_d[l], ssm_w_glu[l], ssm_b_glu[l])
        y_mla = _mla_group(c_q, c_kv, k_pe, positions, mla_q_norm_w[l], mla_w_uq[l],
                           mla_kv_norm_w[l], mla_w_ukv[l])
        y = jnp.concatenate([_rmsnorm(y_ssm, ssm_out_norm_w[l]),
                             _rmsnorm(y_mla, mla_out_norm_w[l])], axis=-1)
        h = h + y @ w_out[l]
        h = h + _conv_ffn(_rmsnorm(h, ffn_norm_w[l]), ffn_w_up[l], ffn_conv_w[l],
                          ffn_conv_b[l], ffn_w_down[l])
    return _rmsnorm(h, final_norm_w)
```

```python
import functools
import math

import numpy as np
import jax
import jax.numpy as jnp
from jax import lax
from jax.experimental import pallas as pl
from jax.experimental.pallas import tpu as pltpu

D_MODEL = 2048
BATCH = 8
SEQ = 2048
TOKENS = BATCH * SEQ
SSM_WIDTH = 1024
SSM_GROUP = 16
SSM_GROUPS = 64
SSM_STATE = 64
QK_NOPE_DIM = 128
QK_ROPE_DIM = 64
V_HEAD_DIM = 128
MLA_WIDTH = 1024
MLA_HEADS = 8
Q_LORA_RANK = 512
KV_LORA_RANK = 256
ROPE_THETA = 10000.0
D_FF = 5632
RMS_EPS = 1e-6

F32 = jnp.float32
BF16 = jnp.bfloat16

LANES = 128
VMEM_LIMIT_BYTES = 56 * 1024 * 1024

IN_ROWS = 512
SSM_TIME = 64
SSM_GB = 16
SSM_NGB = SSM_GROUPS // SSM_GB
SSM_ULANES = SSM_GB * SSM_GROUP
SSM_SLANES = SSM_GB * SSM_STATE
ATT_Q = 256
OUT_ROWS = 512
FFN_ROWS = 512
FFN_COLS = 512
FFN_HALO = 16


def _params(*sem):
    return pltpu.CompilerParams(dimension_semantics=sem, vmem_limit_bytes=VMEM_LIMIT_BYTES)


def _rms(x, w):
    return x * lax.rsqrt(jnp.mean(x * x, axis=-1, keepdims=True) + RMS_EPS) * w


def _sigmoid(x):
    return 1.0 / (1.0 + jnp.exp(-x))


def _zoh_kernel(lr_ref, li_ref, ldt_ref, bre_ref, bim_ref, are_ref, aim_ref, bbre_ref, bbim_ref):
    lr = lr_ref[...]
    li = li_ref[...]
    dt = jnp.exp(ldt_ref[...])
    mag = jnp.exp(lr * dt)
    abar_re = mag * jnp.cos(li * dt)
    abar_im = mag * jnp.sin(li * dt)
    nr, ni = abar_re - 1.0, abar_im
    den = lr * lr + li * li
    zr = (nr * lr + ni * li) / den
    zi = (ni * lr - nr * li) / den
    are_ref[...] = abar_re
    aim_ref[...] = abar_im
    bre = bre_ref[...]
    bim = bim_ref[...]
    bbre_ref[...] = zr[:, None, :] * bre - zi[:, None, :] * bim
    bbim_ref[...] = zr[:, None, :] * bim + zi[:, None, :] * bre


def _zoh(lam_re, lam_im, log_dt, b_re_ghp, b_im_ghp):
    gp = jax.ShapeDtypeStruct((SSM_GROUPS, SSM_STATE), F32)
    ghp = jax.ShapeDtypeStruct((SSM_GROUPS, SSM_GROUP, SSM_STATE), F32)
    return pl.pallas_call(
        _zoh_kernel, out_shape=(gp, gp, ghp, ghp), name="ssm_zoh",
    )(lam_re, lam_im, log_dt.reshape(SSM_GROUPS, 1), b_re_ghp, b_im_ghp)


def _inproj_kernel(x_ref, pos_ref, invf_ref, nw_ref, win_ref, qnw_ref, wuq_ref, kvnw_ref, wukv_ref,
                   u_ref, qn_ref, qp_ref, kn_ref, v_ref, kp_ref):
    x = x_ref[...]
    hn = _rms(x, nw_ref[...])
    proj = jnp.dot(hn.astype(BF16), win_ref[...], preferred_element_type=F32)
    u_ref[...] = proj[:, :SSM_WIDTH].astype(BF16)
    c_q = proj[:, SSM_WIDTH:SSM_WIDTH + Q_LORA_RANK]
    c_kv = proj[:, SSM_WIDTH + Q_LORA_RANK:SSM_WIDTH + Q_LORA_RANK + KV_LORA_RANK]
    kp = proj[:, SSM_WIDTH + Q_LORA_RANK + KV_LORA_RANK:]

    ang = pos_ref[...].astype(F32) * invf_ref[...]
    cos = jnp.cos(ang)
    sin = jnp.sin(ang)
    lane = lax.broadcasted_iota(jnp.int32, (1, LANES), 1)
    low = lane < QK_ROPE_DIM

    y = kp * jnp.where(low, cos, sin)
    r = y + pltpu.roll(y, QK_ROPE_DIM, axis=1)
    zero = jnp.zeros_like(r)
    kp_ref[...] = jnp.concatenate([jnp.where(low, r, zero), jnp.where(low, zero, r)], axis=1).astype(BF16)

    q = jnp.dot(_rms(c_q, qnw_ref[...]).astype(BF16), wuq_ref[...], preferred_element_type=F32)
    npe = MLA_HEADS * QK_ROPE_DIM
    nn = MLA_HEADS * QK_NOPE_DIM
    qn_ref[...] = q[:, :nn].astype(BF16)
    cos4 = jnp.concatenate([cos] * (npe // LANES), axis=1)
    sin4 = jnp.concatenate([sin] * (npe // LANES), axis=1)
    qp_ref[...] = (q[:, nn:nn + npe] * cos4 + q[:, nn + npe:] * sin4).astype(BF16)

    kv = jnp.dot(_rms(c_kv, kvnw_ref[...]).astype(BF16), wukv_ref[...], preferred_element_type=F32)
    kn_ref[...] = kv[:, :nn].astype(BF16)
    v_ref[...] = kv[:, nn:].astype(BF16)


def _inproj(x2d, pos2d, invf, nw, win, qnw, wuq, kvnw, wukv):
    r = IN_ROWS
    row = lambda n: pl.BlockSpec((r, n), lambda i: (i, 0))
    full = lambda a: pl.BlockSpec(a.shape, lambda i: (0,) * a.ndim)
    outs = (SSM_WIDTH, MLA_HEADS * QK_NOPE_DIM, MLA_HEADS * QK_ROPE_DIM,
            MLA_HEADS * QK_NOPE_DIM, MLA_HEADS * V_HEAD_DIM, 2 * LANES)
    return pl.pallas_call(
        _inproj_kernel,
        grid=(TOKENS // r,),
        in_specs=[row(D_MODEL), row(1), full(invf), full(nw), full(win), full(qnw), full(wuq),
                  full(kvnw), full(wukv)],
        out_specs=tuple(row(n) for n in outs),
        out_shape=tuple(jax.ShapeDtypeStruct((TOKENS, n), BF16) for n in outs),
        compiler_params=_params("parallel"),
        name="in_proj",
    )(x2d, pos2d, invf, nw, win, qnw, wuq, kvnw, wukv)


def _gelu_tanh(x):
    c = math.sqrt(2.0 / math.pi)
    return 0.5 * x * (1.0 + jnp.tanh(c * (x + 0.044715 * (x * x * x))))


def _ssm_kernel(u_ref, bblk_ref, cblk_ref, are_ref, aim_ref, d_ref, wglu_ref, bglu_ref, onw_ref,
                out_ref, bu_scr, s_scr, y_scr, st_re, st_im):
    tt = pl.program_id(0)
    gb = pl.program_id(1)

    @pl.when(tt == 0)
    def _():
        st_re[gb] = jnp.zeros((BATCH, SSM_SLANES), F32)
        st_im[gb] = jnp.zeros((BATCH, SSM_SLANES), F32)

    u = u_ref[...]
    bu_scr[...] = jnp.dot(u, bblk_ref[0], preferred_element_type=F32)
    a_re = jnp.broadcast_to(are_ref[0], (BATCH, SSM_SLANES))
    a_im = jnp.broadcast_to(aim_ref[0], (BATCH, SSM_SLANES))
    s_re = st_re[gb]
    s_im = st_im[gb]
    for t in range(SSM_TIME):
        rows = slice(t * BATCH, (t + 1) * BATCH)
        n_re = a_re * s_re - a_im * s_im + bu_scr[rows, :SSM_SLANES]
        n_im = a_re * s_im + a_im * s_re + bu_scr[rows, SSM_SLANES:]
        s_scr[rows, :SSM_SLANES] = n_re
        s_scr[rows, SSM_SLANES:] = n_im
        s_re, s_im = n_re, n_im
    st_re[gb] = s_re
    st_im[gb] = s_im

    y = jnp.dot(s_scr[...].astype(BF16), cblk_ref[0], preferred_element_type=F32)
    y = y + d_ref[0] * u.astype(F32)
    y_scr[gb] = _gelu_tanh(y)

    @pl.when(gb == SSM_NGB - 1)
    def _():
        yf = jnp.concatenate([y_scr[k] for k in range(SSM_NGB)], axis=1)
        z = jnp.dot(yf.astype(BF16), wglu_ref[...], preferred_element_type=F32) + bglu_ref[...]
        g = yf * _sigmoid(z)
        out_ref[...] = _rms(g, onw_ref[...]).astype(BF16)


def _ssm(u_tm, bblk, cblk, are, aim, dskip, wglu, bglu, onw):
    rows = SSM_TIME * BATCH
    full = lambda a: pl.BlockSpec(a.shape, lambda t, g: (0,) * a.ndim)
    per_gb = lambda a: pl.BlockSpec((1,) + a.shape[1:], lambda t, g: (g,) + (0,) * (a.ndim - 1))
    return pl.pallas_call(
        _ssm_kernel,
        grid=(SEQ // SSM_TIME, SSM_NGB),
        in_specs=[pl.BlockSpec((rows, SSM_ULANES), lambda t, g: (t, g)),
                  per_gb(bblk), per_gb(cblk), per_gb(are), per_gb(aim), per_gb(dskip),
                  full(wglu), full(bglu), full(onw)],
        out_specs=pl.BlockSpec((rows, SSM_WIDTH), lambda t, g: (t, 0)),
        out_shape=jax.ShapeDtypeStruct((TOKENS, SSM_WIDTH), BF16),
        scratch_shapes=[pltpu.VMEM((rows, 2 * SSM_SLANES), F32),
                        pltpu.VMEM((rows, 2 * SSM_SLANES), F32),
                        pltpu.VMEM((SSM_NGB, rows, SSM_ULANES), F32),
                        pltpu.VMEM((SSM_NGB, BATCH, SSM_SLANES), F32),
                        pltpu.VMEM((SSM_NGB, BATCH, SSM_SLANES), F32)],
        compiler_params=_params("arbitrary", "arbitrary"),
        name="ssm",
    )(u_tm, bblk, cblk, are, aim, dskip, wglu, bglu, onw)


def _attn_kernel(qn_ref, qp_ref, kn_ref, kp_ref, v_ref, o_ref):
    scale = (QK_NOPE_DIM + QK_ROPE_DIM) ** -0.5
    neg = float(jnp.finfo(jnp.float32).min)
    k = jnp.concatenate([kn_ref[...], kp_ref[...]], axis=1)
    row = lax.broadcasted_iota(jnp.int32, (ATT_Q, ATT_Q), 0)
    col = lax.broadcasted_iota(jnp.int32, (ATT_Q, ATT_Q), 1)
    causal = col <= row
    for qi in range(SEQ // ATT_Q):
        q0 = qi * ATT_Q
        kend = q0 + ATT_Q
        q = jnp.concatenate([qn_ref[q0:kend, :], qp_ref[q0:kend, :]], axis=1)
        s = lax.dot_general(q, k[:kend], (((1,), (1,)), ((), ())), preferred_element_type=F32) * scale
        s_diag = jnp.where(causal, s[:, q0:], neg)
        m = jnp.max(s_diag, axis=-1, keepdims=True)
        if qi:
            s_main = s[:, :q0]
            m = jnp.maximum(m, jnp.max(s_main, axis=-1, keepdims=True))
            p = jnp.concatenate([jnp.exp(s_main - m), jnp.exp(s_diag - m)], axis=1)
        else:
            p = jnp.exp(s_diag - m)
        denom = jnp.sum(p, axis=-1, keepdims=True)
        o = jnp.dot(p.astype(BF16), v_ref[:kend, :], preferred_element_type=F32)
        o_ref[q0:kend, :] = (o / denom).astype(BF16)


def _attn(qn, qp, kn, kp2, v):
    blk = lambda f: pl.BlockSpec((SEQ, LANES), f)
    return pl.pallas_call(
        _attn_kernel,
        grid=(BATCH, MLA_HEADS),
        in_specs=[blk(lambda b, h: (b, h)), blk(lambda b, h: (b, h // 2)), blk(lambda b, h: (b, h)),
                  blk(lambda b, h: (b, h % 2)), blk(lambda b, h: (b, h))],
        out_specs=blk(lambda b, h: (b, h)),
        out_shape=jax.ShapeDtypeStruct((TOKENS, MLA_WIDTH), BF16),
        compiler_params=_params("parallel", "parallel"),
        name="mla_attn",
    )(qn, qp, kn, kp2, v)


def _outproj_kernel(ys_ref, o_ref, x_ref, monw_ref, wtop_ref, wbot_ref, fnw_ref, h_ref, hn_ref):
    on = _rms(o_ref[...].astype(F32), monw_ref[...]).astype(BF16)
    acc = jnp.dot(ys_ref[...], wtop_ref[...], preferred_element_type=F32)
    acc = acc + jnp.dot(on, wbot_ref[...], preferred_element_type=F32)
    h = x_ref[...] + acc
    h_ref[...] = h
    hn_ref[...] = _rms(h, fnw_ref[...]).astype(BF16)


def _outproj(ys, o, x2d, monw, wtop, wbot, fnw):
    r = OUT_ROWS
    row = lambda n: pl.BlockSpec((r, n), lambda i: (i, 0))
    full = lambda a: pl.BlockSpec(a.shape, lambda i: (0,) * a.ndim)
    return pl.pallas_call(
        _outproj_kernel,
        grid=(TOKENS // r,),
        in_specs=[row(SSM_WIDTH), row(MLA_WIDTH), row(D_MODEL), full(monw), full(wtop), full(wbot), full(fnw)],
        out_specs=(row(D_MODEL), row(D_MODEL)),
        out_shape=(jax.ShapeDtypeStruct((TOKENS, D_MODEL), F32),
                   jax.ShapeDtypeStruct((TOKENS, D_MODEL), BF16)),
        compiler_params=_params("parallel"),
        name="out_proj",
    )(ys, o, x2d, monw, wtop, wbot, fnw)


def _ffn_kernel(hn_ref, halo_ref, wg_ref, wv_ref, cwg_ref, cwv_ref, cbg_ref, cbv_ref, wd_ref, h_ref,
                fnw_ref, out_ref, xext, acc):
    i = pl.program_id(0)
    j = pl.program_id(1)

    @pl.when(j == 0)
    def _():
        seq_start = (i % (SEQ // FFN_ROWS)) == 0
        halo = halo_ref[...]
        xext[:FFN_HALO, :] = jnp.where(seq_start, jnp.zeros_like(halo), halo)
        xext[FFN_HALO:, :] = hn_ref[...]
        acc[...] = jnp.zeros_like(acc)

    xe = xext[...]

    def conv(a, cw, cb):
        a1 = pltpu.roll(a, 1, axis=0)
        a2 = pltpu.roll(a, 2, axis=0)
        return (cw[0:1, :] * a2[FFN_HALO:] + cw[1:2, :] * a1[FFN_HALO:] + cw[2:3, :] * a[FFN_HALO:]) + cb

    g = conv(jnp.dot(xe, wg_ref[...], preferred_element_type=F32), cwg_ref[...], cbg_ref[...])
    v = conv(jnp.dot(xe, wv_ref[...], preferred_element_type=F32), cwv_ref[...], cbv_ref[...])
    act = (g * _sigmoid(g)) * v
    acc[...] += jnp.dot(act.astype(BF16), wd_ref[...], preferred_element_type=F32)

    @pl.when(j == pl.num_programs(1) - 1)
    def _():
        out_ref[...] = _rms(h_ref[...] + acc[...], fnw_ref[...])


def _ffn(hn, w_up, conv_w, conv_b, w_down, h, fnw):
    r, c = FFN_ROWS, FFN_COLS
    nj = D_FF // c
    halo_blocks = r // FFN_HALO
    return pl.pallas_call(
        _ffn_kernel,
        grid=(TOKENS // r, nj),
        in_specs=[pl.BlockSpec((r, D_MODEL), lambda i, j: (i, 0)),
                  pl.BlockSpec((FFN_HALO, D_MODEL), lambda i, j: (jnp.maximum(i * halo_blocks - 1, 0), 0)),
                  pl.BlockSpec((D_MODEL, c), lambda i, j: (0, j)),
                  pl.BlockSpec((D_MODEL, c), lambda i, j: (0, nj + j)),
                  pl.BlockSpec((3, c), lambda i, j: (0, j)),
                  pl.BlockSpec((3, c), lambda i, j: (0, nj + j)),
                  pl.BlockSpec((1, c), lambda i, j: (0, j)),
                  pl.BlockSpec((1, c), lambda i, j: (0, nj + j)),
                  pl.BlockSpec((c, D_MODEL), lambda i, j: (j, 0)),
                  pl.BlockSpec((r, D_MODEL), lambda i, j: (i, 0)),
                  pl.BlockSpec((1, D_MODEL), lambda i, j: (0, 0))],
        out_specs=pl.BlockSpec((r, D_MODEL), lambda i, j: (i, 0)),
        out_shape=jax.ShapeDtypeStruct((TOKENS, D_MODEL), F32),
        scratch_shapes=[pltpu.VMEM((FFN_HALO + r, D_MODEL), BF16), pltpu.VMEM((r, D_MODEL), F32)],
        compiler_params=_params("parallel", "arbitrary"),
        name="conv_ffn",
    )(hn, hn, w_up, w_up, conv_w, conv_w, conv_b, conv_b, w_down, h, fnw)


def _rot_cols(w):
    half = QK_ROPE_DIM // 2
    return jnp.concatenate([-w[..., half:], w[..., :half]], axis=-1)


def _block_diag(w):
    eye = jnp.eye(SSM_GB, dtype=w.dtype)
    out = w[:, :, :, None, :] * eye[None, :, None, :, None]
    return out.reshape(w.shape[0], SSM_GB * w.shape[2], SSM_GB * w.shape[3])


def kernel(x, positions, attn_norm_w, w_in, ssm_lambda_re, ssm_lambda_im, ssm_log_dt, ssm_b_re, ssm_b_im,
           ssm_c_re, ssm_c_im, ssm_d, ssm_w_glu, ssm_b_glu, mla_q_norm_w, mla_w_uq, mla_kv_norm_w, mla_w_ukv,
           ssm_out_norm_w, mla_out_norm_w, w_out, ffn_norm_w, ffn_w_up, ffn_conv_w, ffn_conv_b, ffn_w_down,
           final_norm_w):
    l = 0
    x2d = x.reshape(TOKENS, D_MODEL)
    pos2d = positions.reshape(TOKENS, 1)
    row = lambda a: a.reshape(1, -1)

    kpe0 = SSM_WIDTH + Q_LORA_RANK + KV_LORA_RANK
    w_kpe = w_in[l][:, kpe0:]
    win = jnp.concatenate([w_in[l], _rot_cols(w_kpe)], axis=1).astype(BF16)
    wq = mla_w_uq[l].reshape(Q_LORA_RANK, MLA_HEADS, QK_NOPE_DIM + QK_ROPE_DIM)
    wq_pe = wq[:, :, QK_NOPE_DIM:]
    wuq = jnp.concatenate([wq[:, :, :QK_NOPE_DIM].reshape(Q_LORA_RANK, -1),
                           wq_pe.reshape(Q_LORA_RANK, -1),
                           _rot_cols(wq_pe).reshape(Q_LORA_RANK, -1)], axis=1).astype(BF16)
    wkv = mla_w_ukv[l].reshape(KV_LORA_RANK, MLA_HEADS, QK_NOPE_DIM + V_HEAD_DIM)
    wukv = jnp.concatenate([wkv[:, :, :QK_NOPE_DIM].reshape(KV_LORA_RANK, -1),
                            wkv[:, :, QK_NOPE_DIM:].reshape(KV_LORA_RANK, -1)], axis=1).astype(BF16)
    invf = np.asarray(ROPE_THETA ** (-np.arange(0, QK_ROPE_DIM, 2, dtype=np.float64) / QK_ROPE_DIM), np.float32)
    invf = jnp.asarray(np.tile(invf, LANES // (QK_ROPE_DIM // 2)).reshape(1, LANES))

    u, qn, qp, kn, v, kp2 = _inproj(x2d, pos2d, invf, row(attn_norm_w[l]), win, row(mla_q_norm_w[l]), wuq,
                                    row(mla_kv_norm_w[l]), wukv)

    abar_re, abar_im, bbar_re, bbar_im = _zoh(ssm_lambda_re[l], ssm_lambda_im[l], ssm_log_dt[l],
                                              ssm_b_re[l].transpose(0, 2, 1), ssm_b_im[l].transpose(0, 2, 1))
    grp = lambda a: a.reshape((SSM_NGB, SSM_GB) + a.shape[1:])
    bblk = jnp.concatenate([_block_diag(grp(bbar_re)), _block_diag(grp(bbar_im))], axis=2).astype(BF16)
    c_re = grp(ssm_c_re[l].transpose(0, 2, 1))
    c_im = grp(ssm_c_im[l].transpose(0, 2, 1))
    cblk = jnp.concatenate([_block_diag(c_re), _block_diag(-c_im)], axis=1).astype(BF16)
    are = abar_re.reshape(SSM_NGB, 1, SSM_SLANES)
    aim = abar_im.reshape(SSM_NGB, 1, SSM_SLANES)
    dskip = ssm_d[l].reshape(SSM_NGB, 1, SSM_ULANES)
    u_tm = u.reshape(BATCH, SEQ, SSM_WIDTH).transpose(1, 0, 2).reshape(TOKENS, SSM_WIDTH)
    ys_tm = _ssm(u_tm, bblk, cblk, are, aim, dskip, ssm_w_glu[l].astype(BF16), row(ssm_b_glu[l]),
                 row(ssm_out_norm_w[l]))
    ys = ys_tm.reshape(SEQ, BATCH, SSM_WIDTH).transpose(1, 0, 2).reshape(TOKENS, SSM_WIDTH)

    o = _attn(qn, qp, kn, kp2, v)

    wout = w_out[l].astype(BF16)
    h, hn = _outproj(ys, o, x2d, row(mla_out_norm_w[l]), wout[:SSM_WIDTH], wout[SSM_WIDTH:], row(ffn_norm_w[l]))
    out = _ffn(hn, ffn_w_up[l].astype(BF16), ffn_conv_w[l], row(ffn_conv_b[l]), ffn_w_down[l].astype(BF16),
               h, row(final_norm_w))
    return out.reshape(BATCH, SEQ, D_MODEL)
```

```python
import functools
import math

import numpy as np
import jax
import jax.numpy as jnp
from jax import lax
from jax.experimental import pallas as pl
from jax.experimental.pallas import tpu as pltpu

D_MODEL = 2048
BATCH = 8
SEQ = 2048
TOKENS = BATCH * SEQ
SSM_WIDTH = 1024
SSM_GROUP = 16
SSM_GROUPS = 64
SSM_STATE = 64
QK_NOPE_DIM = 128
QK_ROPE_DIM = 64
V_HEAD_DIM = 128
MLA_WIDTH = 1024
MLA_HEADS = 8
Q_LORA_RANK = 512
KV_LORA_RANK = 256
ROPE_THETA = 10000.0
D_FF = 5632
RMS_EPS = 1e-6

F32 = jnp.float32
BF16 = jnp.bfloat16

LANES = 128
VMEM_LIMIT_BYTES = 56 * 1024 * 1024

IN_ROWS = 512
SSM_TIME = 32
SSM_GB = 16
SSM_NGB = SSM_GROUPS // SSM_GB
SSM_ULANES = SSM_GB * SSM_GROUP
SSM_SLANES = SSM_GB * SSM_STATE
ATT_Q = 512
OUT_ROWS = 512
FFN_ROWS = 512
FFN_COLS = 512
CONV_TAIL = 8


def _params(*sem):
    return pltpu.CompilerParams(dimension_semantics=sem, vmem_limit_bytes=VMEM_LIMIT_BYTES)


def _rms(x, w):
    return x * lax.rsqrt(jnp.mean(x * x, axis=-1, keepdims=True) + RMS_EPS) * w


def _sigmoid(x):
    return 1.0 / (1.0 + jnp.exp(-x))


def _zoh_kernel(lr_ref, li_ref, ldt_ref, bre_ref, bim_ref, are_ref, aim_ref, bbre_ref, bbim_ref):
    lr = lr_ref[...]
    li = li_ref[...]
    dt = jnp.exp(ldt_ref[...])
    mag = jnp.exp(lr * dt)
    abar_re = mag * jnp.cos(li * dt)
    abar_im = mag * jnp.sin(li * dt)
    nr, ni = abar_re - 1.0, abar_im
    den = lr * lr + li * li
    zr = (nr * lr + ni * li) / den
    zi = (ni * lr - nr * li) / den
    are_ref[...] = abar_re
    aim_ref[...] = abar_im
    bre = bre_ref[...]
    bim = bim_ref[...]
    bbre_ref[...] = zr[:, None, :] * bre - zi[:, None, :] * bim
    bbim_ref[...] = zr[:, None, :] * bim + zi[:, None, :] * bre


def _zoh(lam_re, lam_im, log_dt, b_re_ghp, b_im_ghp):
    gp = jax.ShapeDtypeStruct((SSM_GROUPS, SSM_STATE), F32)
    ghp = jax.ShapeDtypeStruct((SSM_GROUPS, SSM_GROUP, SSM_STATE), F32)
    return pl.pallas_call(
        _zoh_kernel, out_shape=(gp, gp, ghp, ghp), name="ssm_zoh",
    )(lam_re, lam_im, log_dt.reshape(SSM_GROUPS, 1), b_re_ghp, b_im_ghp)


def _inproj_kernel(x_ref, pos_ref, invf_ref, nw_ref, win_ref, qnw_ref, wuq_ref, kvnw_ref, wukv_ref,
                   u_ref, qn_ref, qp_ref, kn_ref, v_ref, kp_ref):
    x = x_ref[...]
    hn = _rms(x, nw_ref[...])
    proj = jnp.dot(hn.astype(BF16), win_ref[...], preferred_element_type=F32)
    u_ref[...] = proj[:, :SSM_WIDTH].astype(BF16)
    c_q = proj[:, SSM_WIDTH:SSM_WIDTH + Q_LORA_RANK]
    c_kv = proj[:, SSM_WIDTH + Q_LORA_RANK:SSM_WIDTH + Q_LORA_RANK + KV_LORA_RANK]
    kp = proj[:, SSM_WIDTH + Q_LORA_RANK + KV_LORA_RANK:]

    ang = pos_ref[...].astype(F32) * invf_ref[...]
    cos = jnp.cos(ang)
    sin = jnp.sin(ang)
    lane = lax.broadcasted_iota(jnp.int32, (1, LANES), 1)
    low = lane < QK_ROPE_DIM

    y = kp * jnp.where(low, cos, sin)
    r = y + pltpu.roll(y, QK_ROPE_DIM, axis=1)
    zero = jnp.zeros_like(r)
    kp_ref[...] = jnp.concatenate([jnp.where(low, r, zero), jnp.where(low, zero, r)], axis=1).astype(BF16)

    q = jnp.dot(_rms(c_q, qnw_ref[...]).astype(BF16), wuq_ref[...], preferred_element_type=F32)
    npe = MLA_HEADS * QK_ROPE_DIM
    nn = MLA_HEADS * QK_NOPE_DIM
    qn_ref[...] = q[:, :nn].astype(BF16)
    cos4 = jnp.concatenate([cos] * (npe // LANES), axis=1)
    sin4 = jnp.concatenate([sin] * (npe // LANES), axis=1)
    qp_ref[...] = (q[:, nn:nn + npe] * cos4 + q[:, nn + npe:] * sin4).astype(BF16)

    kv = jnp.dot(_rms(c_kv, kvnw_ref[...]).astype(BF16), wukv_ref[...], preferred_element_type=F32)
    kn_ref[...] = kv[:, :nn].astype(BF16)
    v_ref[...] = kv[:, nn:].astype(BF16)


def _inproj(x2d, pos2d, invf, nw, win, qnw, wuq, kvnw, wukv):
    r = IN_ROWS
    row = lambda n: pl.BlockSpec((r, n), lambda i: (i, 0))
    full = lambda a: pl.BlockSpec(a.shape, lambda i: (0,) * a.ndim)
    outs = (SSM_WIDTH, MLA_HEADS * QK_NOPE_DIM, MLA_HEADS * QK_ROPE_DIM,
            MLA_HEADS * QK_NOPE_DIM, MLA_HEADS * V_HEAD_DIM, 2 * LANES)
    return pl.pallas_call(
        _inproj_kernel,
        grid=(TOKENS // r,),
        in_specs=[row(D_MODEL), row(1), full(invf), full(nw), full(win), full(qnw), full(wuq),
                  full(kvnw), full(wukv)],
        out_specs=tuple(row(n) for n in outs),
        out_shape=tuple(jax.ShapeDtypeStruct((TOKENS, n), BF16) for n in outs),
        compiler_params=_params("parallel"),
        name="in_proj",
    )(x2d, pos2d, invf, nw, win, qnw, wuq, kvnw, wukv)


def _gelu_tanh(x):
    c = math.sqrt(2.0 / math.pi)
    return 0.5 * x * (1.0 + jnp.tanh(c * (x + 0.044715 * (x * x * x))))


def _ssm_kernel(u_ref, bblk_ref, cblk_ref, are_ref, aim_ref, d_ref, wglu_ref, bglu_ref, onw_ref,
                out_ref, bu_scr, s_scr, st_re, st_im):
    @pl.when(pl.program_id(0) == 0)
    def _():
        st_re[...] = jnp.zeros_like(st_re)
        st_im[...] = jnp.zeros_like(st_im)

    u = u_ref[...]
    blocks = [slice(g * SSM_ULANES, (g + 1) * SSM_ULANES) for g in range(SSM_NGB)]
    for g, cols in enumerate(blocks):
        bu_scr[g] = jnp.dot(u[:, cols], bblk_ref[g], preferred_element_type=F32)

    ys = []
    for g, cols in enumerate(blocks):
        a_re = jnp.broadcast_to(are_ref[g], (BATCH, SSM_SLANES))
        a_im = jnp.broadcast_to(aim_ref[g], (BATCH, SSM_SLANES))
        s_re = st_re[g]
        s_im = st_im[g]
        for t in range(0, SSM_TIME, 2):
            res, ims = [], []
            for k in range(2):
                rows = slice((t + k) * BATCH, (t + k + 1) * BATCH)
                n_re = a_re * s_re - a_im * s_im + bu_scr[g, rows, :SSM_SLANES]
                n_im = a_re * s_im + a_im * s_re + bu_scr[g, rows, SSM_SLANES:]
                s_re, s_im = n_re, n_im
                res.append(n_re)
                ims.append(n_im)
            pair = slice(t * BATCH, (t + 2) * BATCH)
            s_scr[g, pair, :SSM_SLANES] = jnp.concatenate(res, axis=0).astype(BF16)
            s_scr[g, pair, SSM_SLANES:] = jnp.concatenate(ims, axis=0).astype(BF16)
        st_re[g] = s_re
        st_im[g] = s_im
        y = jnp.dot(s_scr[g], cblk_ref[g], preferred_element_type=F32)
        ys.append(_gelu_tanh(y + d_ref[:, cols] * u[:, cols].astype(F32)))

    yf = jnp.concatenate(ys, axis=1)
    z = jnp.dot(yf.astype(BF16), wglu_ref[...], preferred_element_type=F32) + bglu_ref[...]
    out_ref[...] = _rms(yf * _sigmoid(z), onw_ref[...]).astype(BF16)


def _ssm(u_tm, bblk, cblk, are, aim, dskip, wglu, bglu, onw):
    rows = SSM_TIME * BATCH
    full = lambda a: pl.BlockSpec(a.shape, lambda t: (0,) * a.ndim)
    return pl.pallas_call(
        _ssm_kernel,
        grid=(SEQ // SSM_TIME,),
        in_specs=[pl.BlockSpec((rows, SSM_WIDTH), lambda t: (t, 0)),
                  full(bblk), full(cblk), full(are), full(aim), full(dskip), full(wglu), full(bglu), full(onw)],
        out_specs=pl.BlockSpec((rows, SSM_WIDTH), lambda t: (t, 0)),
        out_shape=jax.ShapeDtypeStruct((TOKENS, SSM_WIDTH), BF16),
        scratch_shapes=[pltpu.VMEM((SSM_NGB, rows, 2 * SSM_SLANES), F32),
                        pltpu.VMEM((SSM_NGB, rows, 2 * SSM_SLANES), BF16),
                        pltpu.VMEM((SSM_NGB, BATCH, SSM_SLANES), F32),
                        pltpu.VMEM((SSM_NGB, BATCH, SSM_SLANES), F32)],
        compiler_params=_params("arbitrary"),
        name="ssm",
    )(u_tm, bblk, cblk, are, aim, dskip, wglu, bglu, onw)


def _attn_kernel(qn_ref, qp_ref, kn_ref, kp_ref, v_ref, o_ref):
    c = (QK_NOPE_DIM + QK_ROPE_DIM) ** -0.5 * math.log2(math.e)
    neg = float(jnp.finfo(jnp.float32).min)
    k = jnp.concatenate([kn_ref[...], kp_ref[...]], axis=1)
    vt = v_ref[...].astype(F32).T.astype(BF16)
    key = lax.broadcasted_iota(jnp.int32, (ATT_Q, ATT_Q), 0)
    qry = lax.broadcasted_iota(jnp.int32, (ATT_Q, ATT_Q), 1)
    causal = key <= qry
    for qi in range(SEQ // ATT_Q):
        q0 = qi * ATT_Q
        kend = q0 + ATT_Q
        q = jnp.concatenate([qn_ref[q0:kend, :], qp_ref[q0:kend, :]], axis=1)
        st = lax.dot_general(k[:kend], q, (((1,), (1,)), ((), ())), preferred_element_type=F32)
        st_diag = jnp.where(causal, st[q0:], neg)
        m = jnp.max(st_diag, axis=0, keepdims=True)
        if qi:
            st_main = st[:q0]
            m = jnp.maximum(m, jnp.max(st_main, axis=0, keepdims=True))
            pt = jnp.concatenate([jnp.exp2((st_main - m) * c), jnp.exp2((st_diag - m) * c)], axis=0)
        else:
            pt = jnp.exp2((st_diag - m) * c)
        denom = jnp.sum(pt, axis=0, keepdims=True)
        ot = jnp.dot(vt[:, :kend], pt.astype(BF16), preferred_element_type=F32)
        o_ref[q0:kend, :] = (ot / denom).T.astype(BF16)


def _attn(qn, qp, kn, kp2, v):
    blk = lambda f: pl.BlockSpec((SEQ, LANES), f)
    return pl.pallas_call(
        _attn_kernel,
        grid=(BATCH, MLA_HEADS),
        in_specs=[blk(lambda b, h: (b, h)), blk(lambda b, h: (b, h // 2)), blk(lambda b, h: (b, h)),
                  blk(lambda b, h: (b, h % 2)), blk(lambda b, h: (b, h))],
        out_specs=blk(lambda b, h: (b, h)),
        out_shape=jax.ShapeDtypeStruct((TOKENS, MLA_WIDTH), BF16),
        compiler_params=_params("parallel", "parallel"),
        name="mla_attn",
    )(qn, qp, kn, kp2, v)


def _outproj_kernel(ys_ref, o_ref, x_ref, monw_ref, wtop_ref, wbot_ref, fnw_ref, h_ref, hn_ref):
    on = _rms(o_ref[...].astype(F32), monw_ref[...]).astype(BF16)
    acc = jnp.dot(ys_ref[...], wtop_ref[...], preferred_element_type=F32)
    acc = acc + jnp.dot(on, wbot_ref[...], preferred_element_type=F32)
    h = x_ref[...] + acc
    h_ref[...] = h
    hn_ref[...] = _rms(h, fnw_ref[...]).astype(BF16)


def _outproj(ys, o, x2d, monw, wtop, wbot, fnw):
    r = OUT_ROWS
    row = lambda n: pl.BlockSpec((r, n), lambda i: (i, 0))
    full = lambda a: pl.BlockSpec(a.shape, lambda i: (0,) * a.ndim)
    return pl.pallas_call(
        _outproj_kernel,
        grid=(TOKENS // r,),
        in_specs=[row(SSM_WIDTH), row(MLA_WIDTH), row(D_MODEL), full(monw), full(wtop), full(wbot), full(fnw)],
        out_specs=(row(D_MODEL), row(D_MODEL)),
        out_shape=(jax.ShapeDtypeStruct((TOKENS, D_MODEL), F32),
                   jax.ShapeDtypeStruct((TOKENS, D_MODEL), BF16)),
        compiler_params=_params("parallel"),
        name="out_proj",
    )(ys, o, x2d, monw, wtop, wbot, fnw)


def _ffn_kernel(hn_ref, wg_ref, wv_ref, cwg_ref, cwv_ref, cbg_ref, cbv_ref, wd_ref, h_ref, fnw_ref,
                out_ref, tail_g, tail_v):
    i = pl.program_id(0)
    j = pl.program_id(1)

    @pl.when(jnp.logical_and(i == 0, j == 0))
    def _():
        tail_g[...] = jnp.zeros_like(tail_g)
        tail_v[...] = jnp.zeros_like(tail_v)

    @pl.when(j == 0)
    def _():
        out_ref[...] = h_ref[...]

    seq_start = (i % (SEQ // FFN_ROWS)) == 0
    x = hn_ref[...]

    def up_conv(w_ref, tail, cw_ref, cb_ref):
        a = jnp.dot(x, w_ref[...], preferred_element_type=F32)
        prev = jnp.where(seq_start, jnp.zeros((CONV_TAIL, FFN_COLS), F32), tail[j])
        tail[j] = a[FFN_ROWS - CONV_TAIL:]
        head = jnp.concatenate([prev, a[:CONV_TAIL]], axis=0)

        def shifted(k):
            return jnp.concatenate([pltpu.roll(head, k, axis=0)[CONV_TAIL:],
                                    pltpu.roll(a, k, axis=0)[CONV_TAIL:]], axis=0)

        cw = cw_ref[...]
        return (cw[0:1, :] * shifted(2) + cw[1:2, :] * shifted(1) + cw[2:3, :] * a) + cb_ref[...]

    g = up_conv(wg_ref, tail_g, cwg_ref, cbg_ref)
    v = up_conv(wv_ref, tail_v, cwv_ref, cbv_ref)
    act = ((g * _sigmoid(g)) * v).astype(BF16)
    out_ref[...] += jnp.dot(act, wd_ref[...], preferred_element_type=F32)

    @pl.when(j == pl.num_programs(1) - 1)
    def _():
        out_ref[...] = _rms(out_ref[...], fnw_ref[...])


def _ffn(hn, w_up, conv_w, conv_b, w_down, h, fnw):
    r, c = FFN_ROWS, FFN_COLS
    nj = D_FF // c
    return pl.pallas_call(
        _ffn_kernel,
        grid=(TOKENS // r, nj),
        in_specs=[pl.BlockSpec((r, D_MODEL), lambda i, j: (i, 0)),
                  pl.BlockSpec((D_MODEL, c), lambda i, j: (0, j)),
                  pl.BlockSpec((D_MODEL, c), lambda i, j: (0, nj + j)),
                  pl.BlockSpec((3, c), lambda i, j: (0, j)),
                  pl.BlockSpec((3, c), lambda i, j: (0, nj + j)),
                  pl.BlockSpec((1, c), lambda i, j: (0, j)),
                  pl.BlockSpec((1, c), lambda i, j: (0, nj + j)),
                  pl.BlockSpec((c, D_MODEL), lambda i, j: (j, 0)),
                  pl.BlockSpec((r, D_MODEL), lambda i, j: (i, 0)),
                  pl.BlockSpec((1, D_MODEL), lambda i, j: (0, 0))],
        out_specs=pl.BlockSpec((r, D_MODEL), lambda i, j: (i, 0)),
        out_shape=jax.ShapeDtypeStruct((TOKENS, D_MODEL), F32),
        scratch_shapes=[pltpu.VMEM((nj, CONV_TAIL, c), F32), pltpu.VMEM((nj, CONV_TAIL, c), F32)],
        compiler_params=_params("arbitrary", "arbitrary"),
        name="conv_ffn",
    )(hn, w_up, w_up, conv_w, conv_w, conv_b, conv_b, w_down, h, fnw)


def _rot_cols(w):
    half = QK_ROPE_DIM // 2
    return jnp.concatenate([-w[..., half:], w[..., :half]], axis=-1)


def _block_diag(w):
    eye = jnp.eye(SSM_GB, dtype=w.dtype)
    out = w[:, :, :, None, :] * eye[None, :, None, :, None]
    return out.reshape(w.shape[0], SSM_GB * w.shape[2], SSM_GB * w.shape[3])


def kernel(x, positions, attn_norm_w, w_in, ssm_lambda_re, ssm_lambda_im, ssm_log_dt, ssm_b_re, ssm_b_im,
           ssm_c_re, ssm_c_im, ssm_d, ssm_w_glu, ssm_b_glu, mla_q_norm_w, mla_w_uq, mla_kv_norm_w, mla_w_ukv,
           ssm_out_norm_w, mla_out_norm_w, w_out, ffn_norm_w, ffn_w_up, ffn_conv_w, ffn_conv_b, ffn_w_down,
           final_norm_w):
    l = 0
    x2d = x.reshape(TOKENS, D_MODEL)
    pos2d = positions.reshape(TOKENS, 1)
    row = lambda a: a.reshape(1, -1)

    kpe0 = SSM_WIDTH + Q_LORA_RANK + KV_LORA_RANK
    w_kpe = w_in[l][:, kpe0:]
    win = jnp.concatenate([w_in[l], _rot_cols(w_kpe)], axis=1).astype(BF16)
    wq = mla_w_uq[l].reshape(Q_LORA_RANK, MLA_HEADS, QK_NOPE_DIM + QK_ROPE_DIM)
    wq_pe = wq[:, :, QK_NOPE_DIM:]
    wuq = jnp.concatenate([wq[:, :, :QK_NOPE_DIM].reshape(Q_LORA_RANK, -1),
                           wq_pe.reshape(Q_LORA_RANK, -1),
                           _rot_cols(wq_pe).reshape(Q_LORA_RANK, -1)], axis=1).astype(BF16)
    wkv = mla_w_ukv[l].reshape(KV_LORA_RANK, MLA_HEADS, QK_NOPE_DIM + V_HEAD_DIM)
    wukv = jnp.concatenate([wkv[:, :, :QK_NOPE_DIM].reshape(KV_LORA_RANK, -1),
                            wkv[:, :, QK_NOPE_DIM:].reshape(KV_LORA_RANK, -1)], axis=1).astype(BF16)
    invf = np.asarray(ROPE_THETA ** (-np.arange(0, QK_ROPE_DIM, 2, dtype=np.float64) / QK_ROPE_DIM), np.float32)
    invf = jnp.asarray(np.tile(invf, LANES // (QK_ROPE_DIM // 2)).reshape(1, LANES))

    u, qn, qp, kn, v, kp2 = _inproj(x2d, pos2d, invf, row(attn_norm_w[l]), win, row(mla_q_norm_w[l]), wuq,
                                    row(mla_kv_norm_w[l]), wukv)

    abar_re, abar_im, bbar_re, bbar_im = _zoh(ssm_lambda_re[l], ssm_lambda_im[l], ssm_log_dt[l],
                                              ssm_b_re[l].transpose(0, 2, 1), ssm_b_im[l].transpose(0, 2, 1))
    grp = lambda a: a.reshape((SSM_NGB, SSM_GB) + a.shape[1:])
    bblk = jnp.concatenate([_block_diag(grp(bbar_re)), _block_diag(grp(bbar_im))], axis=2).astype(BF16)
    c_re = grp(ssm_c_re[l].transpose(0, 2, 1))
    c_im = grp(ssm_c_im[l].transpose(0, 2, 1))
    cblk = jnp.concatenate([_block_diag(c_re), _block_diag(-c_im)], axis=1).astype(BF16)
    are = abar_re.reshape(SSM_NGB, 1, SSM_SLANES)
    aim = abar_im.reshape(SSM_NGB, 1, SSM_SLANES)
    dskip = row(ssm_d[l])
    u_tm = u.reshape(BATCH, SEQ, SSM_WIDTH).transpose(1, 0, 2).reshape(TOKENS, SSM_WIDTH)
    ys_tm = _ssm(u_tm, bblk, cblk, are, aim, dskip, ssm_w_glu[l].astype(BF16), row(ssm_b_glu[l]),
                 row(ssm_out_norm_w[l]))
    ys = ys_tm.reshape(SEQ, BATCH, SSM_WIDTH).transpose(1, 0, 2).reshape(TOKENS, SSM_WIDTH)

    o = _attn(qn, qp, kn, kp2, v)

    wout = w_out[l].astype(BF16)
    h, hn = _outproj(ys, o, x2d, row(mla_out_norm_w[l]), wout[:SSM_WIDTH], wout[SSM_WIDTH:], row(ffn_norm_w[l]))
    out = _ffn(hn, ffn_w_up[l].astype(BF16), ffn_conv_w[l], row(ffn_conv_b[l]), ffn_w_down[l].astype(BF16),
               h, row(final_norm_w))
    return out.reshape(BATCH, SEQ, D_MODEL)
```

```python
import functools
import math

import numpy as np
import jax
import jax.numpy as jnp
from jax import lax
from jax.experimental import pallas as pl
from jax.experimental.pallas import tpu as pltpu

D_MODEL = 2048
BATCH = 8
SEQ = 2048
TOKENS = BATCH * SEQ
SSM_WIDTH = 1024
SSM_GROUP = 16
SSM_GROUPS = 64
SSM_STATE = 64
QK_NOPE_DIM = 128
QK_ROPE_DIM = 64
V_HEAD_DIM = 128
MLA_WIDTH = 1024
MLA_HEADS = 8
Q_LORA_RANK = 512
KV_LORA_RANK = 256
ROPE_THETA = 10000.0
D_FF = 5632
RMS_EPS = 1e-6

F32 = jnp.float32
BF16 = jnp.bfloat16

LANES = 128
VMEM_LIMIT_BYTES = 56 * 1024 * 1024

IN_ROWS = 512
SSM_TIME = 32
SSM_GB = 16
SSM_NGB = SSM_GROUPS // SSM_GB
SSM_ULANES = SSM_GB * SSM_GROUP
SSM_SLANES = SSM_GB * SSM_STATE
ATT_Q = 512
OUT_ROWS = 512
FFN_ROWS = 512
FFN_COLS = 512
CONV_TAIL = 8


def _params(*sem):
    return pltpu.CompilerParams(dimension_semantics=sem, vmem_limit_bytes=VMEM_LIMIT_BYTES)


def _rms(x, w):
    return x * lax.rsqrt(jnp.mean(x * x, axis=-1, keepdims=True) + RMS_EPS) * w


def _sigmoid(x):
    return 1.0 / (1.0 + jnp.exp(-x))


def _zoh_kernel(lr_ref, li_ref, ldt_ref, bre_ref, bim_ref, are_ref, aim_ref, bbre_ref, bbim_ref):
    lr = lr_ref[...]
    li = li_ref[...]
    dt = jnp.exp(ldt_ref[...])
    mag = jnp.exp(lr * dt)
    abar_re = mag * jnp.cos(li * dt)
    abar_im = mag * jnp.sin(li * dt)
    nr, ni = abar_re - 1.0, abar_im
    den = lr * lr + li * li
    zr = (nr * lr + ni * li) / den
    zi = (ni * lr - nr * li) / den
    are_ref[...] = abar_re
    aim_ref[...] = abar_im
    bre = bre_ref[...]
    bim = bim_ref[...]
    bbre_ref[...] = zr[:, None, :] * bre - zi[:, None, :] * bim
    bbim_ref[...] = zr[:, None, :] * bim + zi[:, None, :] * bre


def _zoh(lam_re, lam_im, log_dt, b_re_ghp, b_im_ghp):
    gp = jax.ShapeDtypeStruct((SSM_GROUPS, SSM_STATE), F32)
    ghp = jax.ShapeDtypeStruct((SSM_GROUPS, SSM_GROUP, SSM_STATE), F32)
    return pl.pallas_call(
        _zoh_kernel, out_shape=(gp, gp, ghp, ghp), name="ssm_zoh",
    )(lam_re, lam_im, log_dt.reshape(SSM_GROUPS, 1), b_re_ghp, b_im_ghp)


def _inproj_kernel(x_ref, pos_ref, invf_ref, nw_ref, win_ref, wkp_ref, qnw_ref, wuq_ref, kvnw_ref, wukv_ref,
                   u_ref, qn_ref, qp_ref, kn_ref, v_ref, kp_ref):
    x = x_ref[...]
    hn = _rms(x, nw_ref[...])
    hb = hn.astype(BF16)
    proj = jnp.dot(hb, win_ref[...], preferred_element_type=F32)
    kp = jnp.dot(hb, wkp_ref[...], preferred_element_type=F32)
    u_ref[...] = proj[:, :SSM_WIDTH].astype(BF16)
    c_q = proj[:, SSM_WIDTH:SSM_WIDTH + Q_LORA_RANK]
    c_kv = proj[:, SSM_WIDTH + Q_LORA_RANK:]

    ang = pos_ref[...].astype(F32) * invf_ref[...]
    cos = jnp.cos(ang)
    sin = jnp.sin(ang)
    lane = lax.broadcasted_iota(jnp.int32, (1, LANES), 1)
    low = lane < QK_ROPE_DIM

    y = kp * jnp.where(low, cos, sin)
    r = y + pltpu.roll(y, QK_ROPE_DIM, axis=1)
    zero = jnp.zeros_like(r)
    kp_ref[...] = jnp.concatenate([jnp.where(low, r, zero), jnp.where(low, zero, r)], axis=1).astype(BF16)

    q = jnp.dot(_rms(c_q, qnw_ref[...]).astype(BF16), wuq_ref[...], preferred_element_type=F32)
    npe = MLA_HEADS * QK_ROPE_DIM
    nn = MLA_HEADS * QK_NOPE_DIM
    qn_ref[...] = q[:, :nn].astype(BF16)
    cos4 = jnp.concatenate([cos] * (npe // LANES), axis=1)
    sin4 = jnp.concatenate([sin] * (npe // LANES), axis=1)
    qp_ref[...] = (q[:, nn:nn + npe] * cos4 + q[:, nn + npe:] * sin4).astype(BF16)

    kv = jnp.dot(_rms(c_kv, kvnw_ref[...]).astype(BF16), wukv_ref[...], preferred_element_type=F32)
    kn_ref[...] = kv[:, :nn].astype(BF16)
    v_ref[...] = kv[:, nn:].astype(BF16)


def _inproj(x2d, pos2d, invf, nw, win, wkp, qnw, wuq, kvnw, wukv):
    r = IN_ROWS
    row = lambda n: pl.BlockSpec((r, n), lambda i: (i, 0))
    full = lambda a: pl.BlockSpec(a.shape, lambda i: (0,) * a.ndim)
    outs = (SSM_WIDTH, MLA_HEADS * QK_NOPE_DIM, MLA_HEADS * QK_ROPE_DIM,
            MLA_HEADS * QK_NOPE_DIM, MLA_HEADS * V_HEAD_DIM, 2 * LANES)
    return pl.pallas_call(
        _inproj_kernel,
        grid=(TOKENS // r,),
        in_specs=[row(D_MODEL), row(1), full(invf), full(nw), full(win), full(wkp), full(qnw), full(wuq),
                  full(kvnw), full(wukv)],
        out_specs=tuple(row(n) for n in outs),
        out_shape=tuple(jax.ShapeDtypeStruct((TOKENS, n), BF16) for n in outs),
        compiler_params=_params("parallel"),
        name="in_proj",
    )(x2d, pos2d, invf, nw, win, wkp, qnw, wuq, kvnw, wukv)


def _gelu_tanh(x):
    c = math.sqrt(2.0 / math.pi)
    return 0.5 * x * (1.0 + jnp.tanh(c * (x + 0.044715 * (x * x * x))))


def _ssm_kernel(u_ref, perm_ref, permt_ref, bblk_ref, cblk_ref, are_ref, aim_ref, d_ref, wglu_ref, bglu_ref,
                onw_ref, out_ref, bu_scr, s_scr, st_re, st_im):
    @pl.when(pl.program_id(0) == 0)
    def _():
        st_re[...] = jnp.zeros_like(st_re)
        st_im[...] = jnp.zeros_like(st_im)

    u_bm = u_ref[...].reshape(SSM_TIME * BATCH, SSM_WIDTH)
    u = jnp.dot(perm_ref[...], u_bm, preferred_element_type=F32).astype(BF16)
    blocks = [slice(g * SSM_ULANES, (g + 1) * SSM_ULANES) for g in range(SSM_NGB)]
    for g, cols in enumerate(blocks):
        bu_scr[g] = jnp.dot(u[:, cols], bblk_ref[g], preferred_element_type=F32)

    ys = []
    for g, cols in enumerate(blocks):
        a_re = jnp.broadcast_to(are_ref[g], (BATCH, SSM_SLANES))
        a_im = jnp.broadcast_to(aim_ref[g], (BATCH, SSM_SLANES))
        s_re = st_re[g]
        s_im = st_im[g]
        for t in range(0, SSM_TIME, 2):
            res, ims = [], []
            for k in range(2):
                rows = slice((t + k) * BATCH, (t + k + 1) * BATCH)
                n_re = a_re * s_re - a_im * s_im + bu_scr[g, rows, :SSM_SLANES]
                n_im = a_re * s_im + a_im * s_re + bu_scr[g, rows, SSM_SLANES:]
                s_re, s_im = n_re, n_im
                res.append(n_re)
                ims.append(n_im)
            pair = slice(t * BATCH, (t + 2) * BATCH)
            s_scr[g, pair, :SSM_SLANES] = jnp.concatenate(res, axis=0).astype(BF16)
            s_scr[g, pair, SSM_SLANES:] = jnp.concatenate(ims, axis=0).astype(BF16)
        st_re[g] = s_re
        st_im[g] = s_im
        y = lax.dot_general(s_scr[g], cblk_ref[g], (((1,), (1,)), ((), ())), preferred_element_type=F32)
        ys.append(_gelu_tanh(y + d_ref[:, cols] * u[:, cols].astype(F32)))

    yf = jnp.concatenate(ys, axis=1)
    z = jnp.dot(yf.astype(BF16), wglu_ref[...], preferred_element_type=F32) + bglu_ref[...]
    out_tm = _rms(yf * _sigmoid(z), onw_ref[...]).astype(BF16)
    out_bm = jnp.dot(permt_ref[...], out_tm, preferred_element_type=F32)
    out_ref[...] = out_bm.astype(BF16).reshape(BATCH, SSM_TIME, SSM_WIDTH)


def _ssm(u, perm, permt, bblk, cblk, are, aim, dskip, wglu, bglu, onw):
    rows = SSM_TIME * BATCH
    full = lambda a: pl.BlockSpec(a.shape, lambda t: (0,) * a.ndim)
    seq_blk = pl.BlockSpec((BATCH, SSM_TIME, SSM_WIDTH), lambda t: (0, t, 0))
    return pl.pallas_call(
        _ssm_kernel,
        grid=(SEQ // SSM_TIME,),
        in_specs=[seq_blk, full(perm), full(permt), full(bblk), full(cblk), full(are), full(aim), full(dskip),
                  full(wglu), full(bglu), full(onw)],
        out_specs=seq_blk,
        out_shape=jax.ShapeDtypeStruct((BATCH, SEQ, SSM_WIDTH), BF16),
        scratch_shapes=[pltpu.VMEM((SSM_NGB, rows, 2 * SSM_SLANES), F32),
                        pltpu.VMEM((SSM_NGB, rows, 2 * SSM_SLANES), BF16),
                        pltpu.VMEM((SSM_NGB, BATCH, SSM_SLANES), F32),
                        pltpu.VMEM((SSM_NGB, BATCH, SSM_SLANES), F32)],
        compiler_params=_params("arbitrary"),
        name="ssm",
    )(u, perm, permt, bblk, cblk, are, aim, dskip, wglu, bglu, onw)


def _attn_kernel(qn_ref, qp_ref, kn_ref, kp_ref, v_ref, o_ref):
    c = (QK_NOPE_DIM + QK_ROPE_DIM) ** -0.5 * math.log2(math.e)
    neg = float(jnp.finfo(jnp.float32).min)
    k = jnp.concatenate([kn_ref[...], kp_ref[...]], axis=1)
    vt = v_ref[...].astype(F32).T.astype(BF16)
    key = lax.broadcasted_iota(jnp.int32, (ATT_Q, ATT_Q), 0)
    qry = lax.broadcasted_iota(jnp.int32, (ATT_Q, ATT_Q), 1)
    causal = key <= qry
    for qi in range(SEQ // ATT_Q):
        q0 = qi * ATT_Q
        kend = q0 + ATT_Q
        q = jnp.concatenate([qn_ref[q0:kend, :], qp_ref[q0:kend, :]], axis=1)
        st = lax.dot_general(k[:kend], q, (((1,), (1,)), ((), ())), preferred_element_type=F32)
        st_diag = jnp.where(causal, st[q0:], neg)
        m = jnp.max(st_diag, axis=0, keepdims=True)
        if qi:
            st_main = st[:q0]
            m = jnp.maximum(m, jnp.max(st_main, axis=0, keepdims=True))
            pt = jnp.concatenate([jnp.exp2((st_main - m) * c), jnp.exp2((st_diag - m) * c)], axis=0)
        else:
            pt = jnp.exp2((st_diag - m) * c)
        denom = jnp.sum(pt, axis=0, keepdims=True)
        ot = jnp.dot(vt[:, :kend], pt.astype(BF16), preferred_element_type=F32)
        o_ref[q0:kend, :] = (ot / denom).T.astype(BF16)


def _attn(qn, qp, kn, kp2, v):
    blk = lambda f: pl.BlockSpec((SEQ, LANES), f)
    return pl.pallas_call(
        _attn_kernel,
        grid=(BATCH, MLA_HEADS),
        in_specs=[blk(lambda b, h: (b, h)), blk(lambda b, h: (b, h // 2)), blk(lambda b, h: (b, h)),
                  blk(lambda b, h: (b, h % 2)), blk(lambda b, h: (b, h))],
        out_specs=blk(lambda b, h: (b, h)),
        out_shape=jax.ShapeDtypeStruct((TOKENS, MLA_WIDTH), BF16),
        compiler_params=_params("parallel", "parallel"),
        name="mla_attn",
    )(qn, qp, kn, kp2, v)


def _outproj_kernel(ys_ref, o_ref, x_ref, monw_ref, wtop_ref, wbot_ref, fnw_ref, h_ref, hn_ref):
    on = _rms(o_ref[...].astype(F32), monw_ref[...]).astype(BF16)
    acc = jnp.dot(ys_ref[...], wtop_ref[...], preferred_element_type=F32)
    acc = acc + jnp.dot(on, wbot_ref[...], preferred_element_type=F32)
    h = x_ref[...] + acc
    h_ref[...] = h
    hn_ref[...] = _rms(h, fnw_ref[...]).astype(BF16)


def _outproj(ys, o, x2d, monw, wout, fnw):
    r = OUT_ROWS
    row = lambda n: pl.BlockSpec((r, n), lambda i: (i, 0))
    full = lambda a: pl.BlockSpec(a.shape, lambda i: (0,) * a.ndim)
    return pl.pallas_call(
        _outproj_kernel,
        grid=(TOKENS // r,),
        in_specs=[row(SSM_WIDTH), row(MLA_WIDTH), row(D_MODEL), full(monw),
                  pl.BlockSpec((SSM_WIDTH, D_MODEL), lambda i: (0, 0)),
                  pl.BlockSpec((MLA_WIDTH, D_MODEL), lambda i: (SSM_WIDTH // MLA_WIDTH, 0)), full(fnw)],
        out_specs=(row(D_MODEL), row(D_MODEL)),
        out_shape=(jax.ShapeDtypeStruct((TOKENS, D_MODEL), F32),
                   jax.ShapeDtypeStruct((TOKENS, D_MODEL), BF16)),
        compiler_params=_params("parallel"),
        name="out_proj",
    )(ys, o, x2d, monw, wout, wout, fnw)


def _ffn_kernel(hn_ref, wg_ref, wv_ref, cwg_ref, cwv_ref, cbg_ref, cbv_ref, wd_ref, h_ref, fnw_ref,
                out_ref, tail_g, tail_v):
    i = pl.program_id(0)
    j = pl.program_id(1)

    @pl.when(jnp.logical_and(i == 0, j == 0))
    def _():
        tail_g[...] = jnp.zeros_like(tail_g)
        tail_v[...] = jnp.zeros_like(tail_v)

    @pl.when(j == 0)
    def _():
        out_ref[...] = h_ref[...]

    seq_start = (i % (SEQ // FFN_ROWS)) == 0
    x = hn_ref[...]

    def up_conv(w_ref, tail, cw_ref, cb_ref):
        a = jnp.dot(x, w_ref[...], preferred_element_type=F32)
        prev = jnp.where(seq_start, jnp.zeros((CONV_TAIL, FFN_COLS), F32), tail[j])
        tail[j] = a[FFN_ROWS - CONV_TAIL:]
        head = jnp.concatenate([prev, a[:CONV_TAIL]], axis=0)

        def shifted(k):
            return jnp.concatenate([pltpu.roll(head, k, axis=0)[CONV_TAIL:],
                                    pltpu.roll(a, k, axis=0)[CONV_TAIL:]], axis=0)

        cw = cw_ref[...]
        return (cw[0:1, :] * shifted(2) + cw[1:2, :] * shifted(1) + cw[2:3, :] * a) + cb_ref[...]

    g = up_conv(wg_ref, tail_g, cwg_ref, cbg_ref)
    v = up_conv(wv_ref, tail_v, cwv_ref, cbv_ref)
    act = ((g * _sigmoid(g)) * v).astype(BF16)
    out_ref[...] += jnp.dot(act, wd_ref[...], preferred_element_type=F32)

    @pl.when(j == pl.num_programs(1) - 1)
    def _():
        out_ref[...] = _rms(out_ref[...], fnw_ref[...])


def _ffn(hn, w_up, conv_w, conv_b, w_down, h, fnw):
    r, c = FFN_ROWS, FFN_COLS
    nj = D_FF // c
    return pl.pallas_call(
        _ffn_kernel,
        grid=(TOKENS // r, nj),
        in_specs=[pl.BlockSpec((r, D_MODEL), lambda i, j: (i, 0)),
                  pl.BlockSpec((D_MODEL, c), lambda i, j: (0, j)),
                  pl.BlockSpec((D_MODEL, c), lambda i, j: (0, nj + j)),
                  pl.BlockSpec((3, c), lambda i, j: (0, j)),
                  pl.BlockSpec((3, c), lambda i, j: (0, nj + j)),
                  pl.BlockSpec((1, c), lambda i, j: (0, j)),
                  pl.BlockSpec((1, c), lambda i, j: (0, nj + j)),
                  pl.BlockSpec((c, D_MODEL), lambda i, j: (j, 0)),
                  pl.BlockSpec((r, D_MODEL), lambda i, j: (i, 0)),
                  pl.BlockSpec((1, D_MODEL), lambda i, j: (0, 0))],
        out_specs=pl.BlockSpec((r, D_MODEL), lambda i, j: (i, 0)),
        out_shape=jax.ShapeDtypeStruct((TOKENS, D_MODEL), F32),
        scratch_shapes=[pltpu.VMEM((nj, CONV_TAIL, c), F32), pltpu.VMEM((nj, CONV_TAIL, c), F32)],
        compiler_params=_params("arbitrary", "arbitrary"),
        name="conv_ffn",
    )(hn, w_up, w_up, conv_w, conv_w, conv_b, conv_b, w_down, h, fnw)


def _rot_cols(w):
    half = QK_ROPE_DIM // 2
    return jnp.concatenate([-w[..., half:], w[..., :half]], axis=-1)


def _block_diag(w):
    q = w.reshape(SSM_NGB, SSM_GB, SSM_GROUP, SSM_STATE).transpose(0, 2, 1, 3)
    q = q.reshape(SSM_NGB, SSM_GROUP, SSM_SLANES)
    mask = np.arange(SSM_ULANES)[:, None] // SSM_GROUP == np.arange(SSM_SLANES)[None, :] // SSM_STATE
    return jnp.where(jnp.asarray(mask), jnp.tile(q, (1, SSM_GB, 1)), 0.0)


def kernel(x, positions, attn_norm_w, w_in, ssm_lambda_re, ssm_lambda_im, ssm_log_dt, ssm_b_re, ssm_b_im,
           ssm_c_re, ssm_c_im, ssm_d, ssm_w_glu, ssm_b_glu, mla_q_norm_w, mla_w_uq, mla_kv_norm_w, mla_w_ukv,
           ssm_out_norm_w, mla_out_norm_w, w_out, ffn_norm_w, ffn_w_up, ffn_conv_w, ffn_conv_b, ffn_w_down,
           final_norm_w):
    l = 0
    x2d = x.reshape(TOKENS, D_MODEL)
    pos2d = positions.reshape(TOKENS, 1)
    row = lambda a: a.reshape(1, -1)

    kpe0 = SSM_WIDTH + Q_LORA_RANK + KV_LORA_RANK
    w_kpe = w_in[l][:, kpe0:]
    win = w_in[l][:, :kpe0].astype(BF16)
    wkp = jnp.concatenate([w_kpe, _rot_cols(w_kpe)], axis=1).astype(BF16)
    wq = mla_w_uq[l].reshape(Q_LORA_RANK, MLA_HEADS, QK_NOPE_DIM + QK_ROPE_DIM)
    wq_pe = wq[:, :, QK_NOPE_DIM:]
    wuq = jnp.concatenate([wq[:, :, :QK_NOPE_DIM].reshape(Q_LORA_RANK, -1),
                           wq_pe.reshape(Q_LORA_RANK, -1),
                           _rot_cols(wq_pe).reshape(Q_LORA_RANK, -1)], axis=1).astype(BF16)
    wkv = mla_w_ukv[l].reshape(KV_LORA_RANK, MLA_HEADS, QK_NOPE_DIM + V_HEAD_DIM)
    wukv = jnp.concatenate([wkv[:, :, :QK_NOPE_DIM].reshape(KV_LORA_RANK, -1),
                            wkv[:, :, QK_NOPE_DIM:].reshape(KV_LORA_RANK, -1)], axis=1).astype(BF16)
    invf = np.asarray(ROPE_THETA ** (-np.arange(0, QK_ROPE_DIM, 2, dtype=np.float64) / QK_ROPE_DIM), np.float32)
    invf = jnp.asarray(np.tile(invf, LANES // (QK_ROPE_DIM // 2)).reshape(1, LANES))

    u, qn, qp, kn, v, kp2 = _inproj(x2d, pos2d, invf, row(attn_norm_w[l]), win, wkp, row(mla_q_norm_w[l]), wuq,
                                    row(mla_kv_norm_w[l]), wukv)

    abar_re, abar_im, bbar_re, bbar_im = _zoh(ssm_lambda_re[l], ssm_lambda_im[l], ssm_log_dt[l],
                                              ssm_b_re[l].transpose(0, 2, 1), ssm_b_im[l].transpose(0, 2, 1))
    bblk = jnp.concatenate([_block_diag(bbar_re), _block_diag(bbar_im)], axis=2).astype(BF16)
    cblk = jnp.concatenate([_block_diag(ssm_c_re[l]), _block_diag(-ssm_c_im[l])], axis=2).astype(BF16)
    are = abar_re.reshape(SSM_NGB, 1, SSM_SLANES)
    aim = abar_im.reshape(SSM_NGB, 1, SSM_SLANES)
    r_tm = np.arange(SSM_TIME * BATCH)
    perm = np.zeros((r_tm.size, r_tm.size), np.float32)
    perm[r_tm, (r_tm % BATCH) * SSM_TIME + r_tm // BATCH] = 1.0
    ys = _ssm(u.reshape(BATCH, SEQ, SSM_WIDTH), jnp.asarray(perm, BF16), jnp.asarray(perm.T, BF16), bblk, cblk,
              are, aim, row(ssm_d[l]), ssm_w_glu[l].astype(BF16), row(ssm_b_glu[l]), row(ssm_out_norm_w[l]))
    ys = ys.reshape(TOKENS, SSM_WIDTH)

    o = _attn(qn, qp, kn, kp2, v)

    h, hn = _outproj(ys, o, x2d, row(mla_out_norm_w[l]), w_out[l].astype(BF16), row(ffn_norm_w[l]))
    out = _ffn(hn, ffn_w_up[l].astype(BF16), ffn_conv_w[l], row(ffn_conv_b[l]), ffn_w_down[l].astype(BF16),
               h, row(final_norm_w))
    return out.reshape(BATCH, SEQ, D_MODEL)
```

```python
import functools
import math

import numpy as np
import jax
import jax.numpy as jnp
from jax import lax
from jax.experimental import pallas as pl
from jax.experimental.pallas import tpu as pltpu

D_MODEL = 2048
BATCH = 8
SEQ = 2048
TOKENS = BATCH * SEQ
SSM_WIDTH = 1024
SSM_GROUP = 16
SSM_GROUPS = 64
SSM_STATE = 64
QK_NOPE_DIM = 128
QK_ROPE_DIM = 64
V_HEAD_DIM = 128
MLA_WIDTH = 1024
MLA_HEADS = 8
Q_LORA_RANK = 512
KV_LORA_RANK = 256
ROPE_THETA = 10000.0
D_FF = 5632
RMS_EPS = 1e-6

F32 = jnp.float32
BF16 = jnp.bfloat16

LANES = 128
VMEM_LIMIT_BYTES = 56 * 1024 * 1024

IN_ROWS = 512
SSM_TIME = 32
SSM_GB = 16
SSM_NGB = SSM_GROUPS // SSM_GB
SSM_ULANES = SSM_GB * SSM_GROUP
SSM_SLANES = SSM_GB * SSM_STATE
ATT_Q = 512
OUT_ROWS = 512
FFN_ROWS = 1024
FFN_COLS = 512
CONV_TAIL = 8


def _params(*sem):
    return pltpu.CompilerParams(dimension_semantics=sem, vmem_limit_bytes=VMEM_LIMIT_BYTES)


def _rms(x, w):
    return x * lax.rsqrt(jnp.mean(x * x, axis=-1, keepdims=True) + RMS_EPS) * w


def _sigmoid(x):
    return 1.0 / (1.0 + jnp.exp(-x))


def _zoh_kernel(lr_ref, li_ref, ldt_ref, bre_ref, bim_ref, are_ref, aim_ref, bbre_ref, bbim_ref):
    lr = lr_ref[...]
    li = li_ref[...]
    dt = jnp.exp(ldt_ref[...])
    mag = jnp.exp(lr * dt)
    abar_re = mag * jnp.cos(li * dt)
    abar_im = mag * jnp.sin(li * dt)
    nr, ni = abar_re - 1.0, abar_im
    den = lr * lr + li * li
    zr = (nr * lr + ni * li) / den
    zi = (ni * lr - nr * li) / den
    are_ref[...] = abar_re
    aim_ref[...] = abar_im
    bre = bre_ref[...]
    bim = bim_ref[...]
    bbre_ref[...] = zr[:, None, :] * bre - zi[:, None, :] * bim
    bbim_ref[...] = zr[:, None, :] * bim + zi[:, None, :] * bre


def _zoh(lam_re, lam_im, log_dt, b_re_ghp, b_im_ghp):
    gp = jax.ShapeDtypeStruct((SSM_GROUPS, SSM_STATE), F32)
    ghp = jax.ShapeDtypeStruct((SSM_GROUPS, SSM_GROUP, SSM_STATE), F32)
    return pl.pallas_call(
        _zoh_kernel, out_shape=(gp, gp, ghp, ghp), name="ssm_zoh",
    )(lam_re, lam_im, log_dt.reshape(SSM_GROUPS, 1), b_re_ghp, b_im_ghp)


def _inproj_kernel(x_ref, pos_ref, invf_ref, nw_ref, win_ref, wkp_ref, qnw_ref, wuq_ref, kvnw_ref, wukv_ref,
                   u_ref, qn_ref, qp_ref, kn_ref, v_ref, kp_ref):
    x = x_ref[...]
    hn = _rms(x, nw_ref[...])
    hb = hn.astype(BF16)
    proj = jnp.dot(hb, win_ref[...], preferred_element_type=F32)
    kp = jnp.dot(hb, wkp_ref[...], preferred_element_type=F32)
    u_ref[...] = proj[:, :SSM_WIDTH].astype(BF16)
    c_q = proj[:, SSM_WIDTH:SSM_WIDTH + Q_LORA_RANK]
    c_kv = proj[:, SSM_WIDTH + Q_LORA_RANK:]

    ang = pos_ref[...].astype(F32) * invf_ref[...]
    cos = jnp.cos(ang)
    sin = jnp.sin(ang)
    lane = lax.broadcasted_iota(jnp.int32, (1, LANES), 1)
    low = lane < QK_ROPE_DIM

    y = kp * jnp.where(low, cos, sin)
    r = y + pltpu.roll(y, QK_ROPE_DIM, axis=1)
    zero = jnp.zeros_like(r)
    kp_ref[...] = jnp.concatenate([jnp.where(low, r, zero), jnp.where(low, zero, r)], axis=1).astype(BF16)

    q = jnp.dot(_rms(c_q, qnw_ref[...]).astype(BF16), wuq_ref[...], preferred_element_type=F32)
    npe = MLA_HEADS * QK_ROPE_DIM
    nn = MLA_HEADS * QK_NOPE_DIM
    qn_ref[...] = q[:, :nn].astype(BF16)
    cos4 = jnp.concatenate([cos] * (npe // LANES), axis=1)
    sin4 = jnp.concatenate([sin] * (npe // LANES), axis=1)
    qp_ref[...] = (q[:, nn:nn + npe] * cos4 + q[:, nn + npe:] * sin4).astype(BF16)

    kv = jnp.dot(_rms(c_kv, kvnw_ref[...]).astype(BF16), wukv_ref[...], preferred_element_type=F32)
    kn_ref[...] = kv[:, :nn].astype(BF16)
    v_ref[...] = kv[:, nn:].astype(BF16)


def _inproj(x2d, pos2d, invf, nw, win, wkp, qnw, wuq, kvnw, wukv):
    r = IN_ROWS
    row = lambda n: pl.BlockSpec((r, n), lambda i: (i, 0))
    full = lambda a: pl.BlockSpec(a.shape, lambda i: (0,) * a.ndim)
    outs = (SSM_WIDTH, MLA_HEADS * QK_NOPE_DIM, MLA_HEADS * QK_ROPE_DIM,
            MLA_HEADS * QK_NOPE_DIM, MLA_HEADS * V_HEAD_DIM, 2 * LANES)
    return pl.pallas_call(
        _inproj_kernel,
        grid=(TOKENS // r,),
        in_specs=[row(D_MODEL), row(1), full(invf), full(nw), full(win), full(wkp), full(qnw), full(wuq),
                  full(kvnw), full(wukv)],
        out_specs=tuple(row(n) for n in outs),
        out_shape=tuple(jax.ShapeDtypeStruct((TOKENS, n), BF16) for n in outs),
        compiler_params=_params("parallel"),
        name="in_proj",
    )(x2d, pos2d, invf, nw, win, wkp, qnw, wuq, kvnw, wukv)


def _gelu_tanh(x):
    c = math.sqrt(2.0 / math.pi)
    return 0.5 * x * (1.0 + jnp.tanh(c * (x + 0.044715 * (x * x * x))))


def _ssm_kernel(u_ref, perm_ref, permt_ref, bblk_ref, cblk_ref, are_ref, aim_ref, d_ref, wglu_ref, bglu_ref,
                onw_ref, out_ref, bu_scr, s_scr, st_re, st_im):
    @pl.when(pl.program_id(0) == 0)
    def _():
        st_re[...] = jnp.zeros_like(st_re)
        st_im[...] = jnp.zeros_like(st_im)

    u_bm = u_ref[...].reshape(SSM_TIME * BATCH, SSM_WIDTH)
    u = jnp.dot(perm_ref[...], u_bm, preferred_element_type=F32).astype(BF16)
    blocks = [slice(g * SSM_ULANES, (g + 1) * SSM_ULANES) for g in range(SSM_NGB)]
    for g, cols in enumerate(blocks):
        bu_scr[g] = jnp.dot(u[:, cols], bblk_ref[g], preferred_element_type=F32)

    ys = []
    for g, cols in enumerate(blocks):
        a_re = jnp.broadcast_to(are_ref[g], (BATCH, SSM_SLANES))
        a_im = jnp.broadcast_to(aim_ref[g], (BATCH, SSM_SLANES))
        s_re = st_re[g]
        s_im = st_im[g]
        for t in range(0, SSM_TIME, 2):
            res, ims = [], []
            for k in range(2):
                rows = slice((t + k) * BATCH, (t + k + 1) * BATCH)
                n_re = a_re * s_re - a_im * s_im + bu_scr[g, rows, :SSM_SLANES]
                n_im = a_re * s_im + a_im * s_re + bu_scr[g, rows, SSM_SLANES:]
                s_re, s_im = n_re, n_im
                res.append(n_re)
                ims.append(n_im)
            pair = slice(t * BATCH, (t + 2) * BATCH)
            s_scr[g, pair, :SSM_SLANES] = jnp.concatenate(res, axis=0).astype(BF16)
            s_scr[g, pair, SSM_SLANES:] = jnp.concatenate(ims, axis=0).astype(BF16)
        st_re[g] = s_re
        st_im[g] = s_im
        y = lax.dot_general(s_scr[g], cblk_ref[g], (((1,), (1,)), ((), ())), preferred_element_type=F32)
        ys.append(_gelu_tanh(y + d_ref[:, cols] * u[:, cols].astype(F32)))

    yf = jnp.concatenate(ys, axis=1)
    z = jnp.dot(yf.astype(BF16), wglu_ref[...], preferred_element_type=F32) + bglu_ref[...]
    out_tm = _rms(yf * _sigmoid(z), onw_ref[...]).astype(BF16)
    out_bm = jnp.dot(permt_ref[...], out_tm, preferred_element_type=F32)
    out_ref[...] = out_bm.astype(BF16).reshape(BATCH, SSM_TIME, SSM_WIDTH)


def _ssm(u, perm, permt, bblk, cblk, are, aim, dskip, wglu, bglu, onw):
    rows = SSM_TIME * BATCH
    full = lambda a: pl.BlockSpec(a.shape, lambda t: (0,) * a.ndim)
    seq_blk = pl.BlockSpec((BATCH, SSM_TIME, SSM_WIDTH), lambda t: (0, t, 0))
    return pl.pallas_call(
        _ssm_kernel,
        grid=(SEQ // SSM_TIME,),
        in_specs=[seq_blk, full(perm), full(permt), full(bblk), full(cblk), full(are), full(aim), full(dskip),
                  full(wglu), full(bglu), full(onw)],
        out_specs=seq_blk,
        out_shape=jax.ShapeDtypeStruct((BATCH, SEQ, SSM_WIDTH), BF16),
        scratch_shapes=[pltpu.VMEM((SSM_NGB, rows, 2 * SSM_SLANES), F32),
                        pltpu.VMEM((SSM_NGB, rows, 2 * SSM_SLANES), BF16),
                        pltpu.VMEM((SSM_NGB, BATCH, SSM_SLANES), F32),
                        pltpu.VMEM((SSM_NGB, BATCH, SSM_SLANES), F32)],
        compiler_params=_params("arbitrary"),
        name="ssm",
    )(u, perm, permt, bblk, cblk, are, aim, dskip, wglu, bglu, onw)


def _attn_kernel(qn_ref, qp_ref, kn_ref, kp_ref, v_ref, o_ref):
    c = (QK_NOPE_DIM + QK_ROPE_DIM) ** -0.5 * math.log2(math.e)
    neg = float(jnp.finfo(jnp.float32).min)
    k = jnp.concatenate([kn_ref[...], kp_ref[...]], axis=1)
    vt = v_ref[...].astype(F32).T.astype(BF16)
    key = lax.broadcasted_iota(jnp.int32, (ATT_Q, ATT_Q), 0)
    qry = lax.broadcasted_iota(jnp.int32, (ATT_Q, ATT_Q), 1)
    causal = key <= qry

    def scores(qi):
        q0 = qi * ATT_Q
        q = jnp.concatenate([qn_ref[q0:q0 + ATT_Q, :], qp_ref[q0:q0 + ATT_Q, :]], axis=1)
        return lax.dot_general(k[:q0 + ATT_Q], q, (((1,), (1,)), ((), ())), preferred_element_type=F32)

    nq = SEQ // ATT_Q
    st_next = scores(0)
    for qi in range(nq):
        q0 = qi * ATT_Q
        kend = q0 + ATT_Q
        st = st_next
        if qi + 1 < nq:
            st_next = scores(qi + 1)
        st_diag = jnp.where(causal, st[q0:], neg)
        m = jnp.max(st_diag, axis=0, keepdims=True)
        if qi:
            st_main = st[:q0]
            m = jnp.maximum(m, jnp.max(st_main, axis=0, keepdims=True))
            pt = jnp.concatenate([jnp.exp2((st_main - m) * c), jnp.exp2((st_diag - m) * c)], axis=0)
        else:
            pt = jnp.exp2((st_diag - m) * c)
        denom = jnp.sum(pt, axis=0, keepdims=True)
        ot = jnp.dot(vt[:, :kend], pt.astype(BF16), preferred_element_type=F32)
        o_ref[q0:kend, :] = (ot / denom).T.astype(BF16)


def _attn(qn, qp, kn, kp2, v):
    blk = lambda f: pl.BlockSpec((SEQ, LANES), f)
    return pl.pallas_call(
        _attn_kernel,
        grid=(BATCH, MLA_HEADS),
        in_specs=[blk(lambda b, h: (b, h)), blk(lambda b, h: (b, h // 2)), blk(lambda b, h: (b, h)),
                  blk(lambda b, h: (b, h % 2)), blk(lambda b, h: (b, h))],
        out_specs=blk(lambda b, h: (b, h)),
        out_shape=jax.ShapeDtypeStruct((TOKENS, MLA_WIDTH), BF16),
        compiler_params=_params("parallel", "parallel"),
        name="mla_attn",
    )(qn, qp, kn, kp2, v)


def _outproj_kernel(ys_ref, o_ref, x_ref, monw_ref, wtop_ref, wbot_ref, fnw_ref, h_ref, hn_ref):
    on = _rms(o_ref[...].astype(F32), monw_ref[...]).astype(BF16)
    acc = jnp.dot(ys_ref[...], wtop_ref[...], preferred_element_type=F32)
    acc = acc + jnp.dot(on, wbot_ref[...], preferred_element_type=F32)
    h = x_ref[...] + acc
    h_ref[...] = h
    hn_ref[...] = _rms(h, fnw_ref[...]).astype(BF16)


def _outproj(ys, o, x2d, monw, wout, fnw):
    r = OUT_ROWS
    row = lambda n: pl.BlockSpec((r, n), lambda i: (i, 0))
    full = lambda a: pl.BlockSpec(a.shape, lambda i: (0,) * a.ndim)
    return pl.pallas_call(
        _outproj_kernel,
        grid=(TOKENS // r,),
        in_specs=[row(SSM_WIDTH), row(MLA_WIDTH), row(D_MODEL), full(monw),
                  pl.BlockSpec((SSM_WIDTH, D_MODEL), lambda i: (0, 0)),
                  pl.BlockSpec((MLA_WIDTH, D_MODEL), lambda i: (SSM_WIDTH // MLA_WIDTH, 0)), full(fnw)],
        out_specs=(row(D_MODEL), row(D_MODEL)),
        out_shape=(jax.ShapeDtypeStruct((TOKENS, D_MODEL), F32),
                   jax.ShapeDtypeStruct((TOKENS, D_MODEL), BF16)),
        compiler_params=_params("parallel"),
        name="out_proj",
    )(ys, o, x2d, monw, wout, wout, fnw)


def _ffn_kernel(hn_ref, wg_ref, wv_ref, cwg_ref, cwv_ref, cbg_ref, cbv_ref, wd_ref, h_hbm, fnw_ref,
                out_ref, tail_g, tail_v, h_sem):
    i = pl.program_id(0)
    j = pl.program_id(1)

    @pl.when(jnp.logical_and(i == 0, j == 0))
    def _():
        tail_g[...] = jnp.zeros_like(tail_g)
        tail_v[...] = jnp.zeros_like(tail_v)

    def residual_copy():
        rows = pl.ds(pl.multiple_of(i * FFN_ROWS, FFN_ROWS), FFN_ROWS)
        return pltpu.make_async_copy(h_hbm.at[rows, :], out_ref, h_sem)

    def body(first):
        if first:
            residual_copy().start()
        seq_start = (i % (SEQ // FFN_ROWS)) == 0
        x = hn_ref[...]

        def up_conv(w_ref, tail, cw_ref, cb_ref):
            a = jnp.dot(x, w_ref[...], preferred_element_type=F32)
            prev = jnp.where(seq_start, jnp.zeros((CONV_TAIL, FFN_COLS), F32), tail[j])
            tail[j] = a[FFN_ROWS - CONV_TAIL:]
            head = jnp.concatenate([prev, a[:CONV_TAIL]], axis=0)

            def shifted(k):
                return jnp.concatenate([pltpu.roll(head, k, axis=0)[CONV_TAIL:],
                                        pltpu.roll(a, k, axis=0)[CONV_TAIL:]], axis=0)

            cw = cw_ref[...]
            return (cw[0:1, :] * shifted(2) + cw[1:2, :] * shifted(1) + cw[2:3, :] * a) + cb_ref[...]

        g = up_conv(wg_ref, tail_g, cwg_ref, cbg_ref)
        v = up_conv(wv_ref, tail_v, cwv_ref, cbv_ref)
        act = ((g * _sigmoid(g)) * v).astype(BF16)
        if first:
            residual_copy().wait()
        out_ref[...] += jnp.dot(act, wd_ref[...], preferred_element_type=F32)

    pl.when(j == 0)(functools.partial(body, True))
    pl.when(j > 0)(functools.partial(body, False))

    @pl.when(j == pl.num_programs(1) - 1)
    def _():
        out_ref[...] = _rms(out_ref[...], fnw_ref[...])


def _ffn(hn, w_up, conv_w, conv_b, w_down, h, fnw):
    r, c = FFN_ROWS, FFN_COLS
    nj = D_FF // c
    return pl.pallas_call(
        _ffn_kernel,
        grid=(TOKENS // r, nj),
        in_specs=[pl.BlockSpec((r, D_MODEL), lambda i, j: (i, 0)),
                  pl.BlockSpec((D_MODEL, c), lambda i, j: (0, j)),
                  pl.BlockSpec((D_MODEL, c), lambda i, j: (0, nj + j)),
                  pl.BlockSpec((3, c), lambda i, j: (0, j)),
                  pl.BlockSpec((3, c), lambda i, j: (0, nj + j)),
                  pl.BlockSpec((1, c), lambda i, j: (0, j)),
                  pl.BlockSpec((1, c), lambda i, j: (0, nj + j)),
                  pl.BlockSpec((c, D_MODEL), lambda i, j: (j, 0)),
                  pl.BlockSpec(memory_space=pl.ANY),
                  pl.BlockSpec((1, D_MODEL), lambda i, j: (0, 0))],
        out_specs=pl.BlockSpec((r, D_MODEL), lambda i, j: (i, 0)),
        out_shape=jax.ShapeDtypeStruct((TOKENS, D_MODEL), F32),
        scratch_shapes=[pltpu.VMEM((nj, CONV_TAIL, c), F32), pltpu.VMEM((nj, CONV_TAIL, c), F32),
                        pltpu.SemaphoreType.DMA(())],
        compiler_params=_params("arbitrary", "arbitrary"),
        name="conv_ffn",
    )(hn, w_up, w_up, conv_w, conv_w, conv_b, conv_b, w_down, h, fnw)


def _rot_cols(w):
    half = QK_ROPE_DIM // 2
    return jnp.concatenate([-w[..., half:], w[..., :half]], axis=-1)


def _block_diag(w):
    q = w.reshape(SSM_NGB, SSM_GB, SSM_GROUP, SSM_STATE).transpose(0, 2, 1, 3)
    q = q.reshape(SSM_NGB, SSM_GROUP, SSM_SLANES)
    mask = np.arange(SSM_ULANES)[:, None] // SSM_GROUP == np.arange(SSM_SLANES)[None, :] // SSM_STATE
    return jnp.where(jnp.asarray(mask), jnp.tile(q, (1, SSM_GB, 1)), 0.0)


def kernel(x, positions, attn_norm_w, w_in, ssm_lambda_re, ssm_lambda_im, ssm_log_dt, ssm_b_re, ssm_b_im,
           ssm_c_re, ssm_c_im, ssm_d, ssm_w_glu, ssm_b_glu, mla_q_norm_w, mla_w_uq, mla_kv_norm_w, mla_w_ukv,
           ssm_out_norm_w, mla_out_norm_w, w_out, ffn_norm_w, ffn_w_up, ffn_conv_w, ffn_conv_b, ffn_w_down,
           final_norm_w):
    l = 0
    x2d = x.reshape(TOKENS, D_MODEL)
    pos2d = positions.reshape(TOKENS, 1)
    row = lambda a: a.reshape(1, -1)

    kpe0 = SSM_WIDTH + Q_LORA_RANK + KV_LORA_RANK
    w_kpe = w_in[l][:, kpe0:]
    win = w_in[l][:, :kpe0].astype(BF16)
    wkp = jnp.concatenate([w_kpe, _rot_cols(w_kpe)], axis=1).astype(BF16)
    wq = mla_w_uq[l].reshape(Q_LORA_RANK, MLA_HEADS, QK_NOPE_DIM + QK_ROPE_DIM)
    wq_pe = wq[:, :, QK_NOPE_DIM:]
    wuq = jnp.concatenate([wq[:, :, :QK_NOPE_DIM].reshape(Q_LORA_RANK, -1),
                           wq_pe.reshape(Q_LORA_RANK, -1),
                           _rot_cols(wq_pe).reshape(Q_LORA_RANK, -1)], axis=1).astype(BF16)
    wkv = mla_w_ukv[l].reshape(KV_LORA_RANK, MLA_HEADS, QK_NOPE_DIM + V_HEAD_DIM)
    wukv = jnp.concatenate([wkv[:, :, :QK_NOPE_DIM].reshape(KV_LORA_RANK, -1),
                            wkv[:, :, QK_NOPE_DIM:].reshape(KV_LORA_RANK, -1)], axis=1).astype(BF16)
    invf = np.asarray(ROPE_THETA ** (-np.arange(0, QK_ROPE_DIM, 2, dtype=np.float64) / QK_ROPE_DIM), np.float32)
    invf = jnp.asarray(np.tile(invf, LANES // (QK_ROPE_DIM // 2)).reshape(1, LANES))

    u, qn, qp, kn, v, kp2 = _inproj(x2d, pos2d, invf, row(attn_norm_w[l]), win, wkp, row(mla_q_norm_w[l]), wuq,
                                    row(mla_kv_norm_w[l]), wukv)

    abar_re, abar_im, bbar_re, bbar_im = _zoh(ssm_lambda_re[l], ssm_lambda_im[l], ssm_log_dt[l],
                                              ssm_b_re[l].transpose(0, 2, 1), ssm_b_im[l].transpose(0, 2, 1))
    bblk = jnp.concatenate([_block_diag(bbar_re), _block_diag(bbar_im)], axis=2).astype(BF16)
    cblk = jnp.concatenate([_block_diag(ssm_c_re[l]), _block_diag(-ssm_c_im[l])], axis=2).astype(BF16)
    are = abar_re.reshape(SSM_NGB, 1, SSM_SLANES)
    aim = abar_im.reshape(SSM_NGB, 1, SSM_SLANES)
    r_tm = np.arange(SSM_TIME * BATCH)
    perm = np.zeros((r_tm.size, r_tm.size), np.float32)
    perm[r_tm, (r_tm % BATCH) * SSM_TIME + r_tm // BATCH] = 1.0
    ys = _ssm(u.reshape(BATCH, SEQ, SSM_WIDTH), jnp.asarray(perm, BF16), jnp.asarray(perm.T, BF16), bblk, cblk,
              are, aim, row(ssm_d[l]), ssm_w_glu[l].astype(BF16), row(ssm_b_glu[l]), row(ssm_out_norm_w[l]))
    ys = ys.reshape(TOKENS, SSM_WIDTH)

    o = _attn(qn, qp, kn, kp2, v)

    h, hn = _outproj(ys, o, x2d, row(mla_out_norm_w[l]), w_out[l].astype(BF16), row(ffn_norm_w[l]))
    out = _ffn(hn, ffn_w_up[l].astype(BF16), ffn_conv_w[l], row(ffn_conv_b[l]), ffn_w_down[l].astype(BF16),
               h, row(final_norm_w))
    return out.reshape(BATCH, SEQ, D_MODEL)
```

```python
import functools
import math

import numpy as np
import jax
import jax.numpy as jnp
from jax import lax
from jax.experimental import pallas as pl
from jax.experimental.pallas import tpu as pltpu

D_MODEL = 2048
BATCH = 8
SEQ = 2048
TOKENS = BATCH * SEQ
SSM_WIDTH = 1024
SSM_GROUP = 16
SSM_GROUPS = 64
SSM_STATE = 64
QK_NOPE_DIM = 128
QK_ROPE_DIM = 64
V_HEAD_DIM = 128
MLA_WIDTH = 1024
MLA_HEADS = 8
Q_LORA_RANK = 512
KV_LORA_RANK = 256
ROPE_THETA = 10000.0
D_FF = 5632
RMS_EPS = 1e-6

F32 = jnp.float32
BF16 = jnp.bfloat16

LANES = 128
VMEM_LIMIT_BYTES = 56 * 1024 * 1024

IN_ROWS = 512
SSM_TIME = 32
SSM_GB = 16
SSM_NGB = SSM_GROUPS // SSM_GB
SSM_ULANES = SSM_GB * SSM_GROUP
SSM_SLANES = SSM_GB * SSM_STATE
ATT_Q = 512
OUT_ROWS = 512
FFN_ROWS = 1024
FFN_COLS = 512
CONV_TAIL = 8


def _params(*sem):
    return pltpu.CompilerParams(dimension_semantics=sem, vmem_limit_bytes=VMEM_LIMIT_BYTES)


def _rms(x, w):
    return x * lax.rsqrt(jnp.mean(x * x, axis=-1, keepdims=True) + RMS_EPS) * w


def _sigmoid(x):
    return 1.0 / (1.0 + jnp.exp(-x))


def _zoh_kernel(lr_ref, li_ref, ldt_ref, bre_ref, bim_ref, are_ref, aim_ref, bbre_ref, bbim_ref):
    lr = lr_ref[...]
    li = li_ref[...]
    dt = jnp.exp(ldt_ref[...])
    mag = jnp.exp(lr * dt)
    abar_re = mag * jnp.cos(li * dt)
    abar_im = mag * jnp.sin(li * dt)
    nr, ni = abar_re - 1.0, abar_im
    den = lr * lr + li * li
    zr = (nr * lr + ni * li) / den
    zi = (ni * lr - nr * li) / den
    are_ref[...] = abar_re
    aim_ref[...] = abar_im
    bre = bre_ref[...]
    bim = bim_ref[...]
    bbre_ref[...] = zr[:, None, :] * bre - zi[:, None, :] * bim
    bbim_ref[...] = zr[:, None, :] * bim + zi[:, None, :] * bre


def _zoh(lam_re, lam_im, log_dt, b_re_ghp, b_im_ghp):
    gp = jax.ShapeDtypeStruct((SSM_GROUPS, SSM_STATE), F32)
    ghp = jax.ShapeDtypeStruct((SSM_GROUPS, SSM_GROUP, SSM_STATE), F32)
    return pl.pallas_call(
        _zoh_kernel, out_shape=(gp, gp, ghp, ghp), name="ssm_zoh",
    )(lam_re, lam_im, log_dt.reshape(SSM_GROUPS, 1), b_re_ghp, b_im_ghp)


def _inproj_kernel(x_ref, pos_ref, invf_ref, nw_ref, win_ref, wkp_ref, qnw_ref, wuq_ref, kvnw_ref, wukv_ref,
                   u_ref, qn_ref, qp_ref, kn_ref, v_ref, kp_ref):
    x = x_ref[...]
    hn = _rms(x, nw_ref[...])
    hb = hn.astype(BF16)
    proj = jnp.dot(hb, win_ref[...], preferred_element_type=F32)
    kp = jnp.dot(hb, wkp_ref[...], preferred_element_type=F32)
    u_ref[...] = proj[:, :SSM_WIDTH].astype(BF16)
    c_q = proj[:, SSM_WIDTH:SSM_WIDTH + Q_LORA_RANK]
    c_kv = proj[:, SSM_WIDTH + Q_LORA_RANK:]

    ang = pos_ref[...].astype(F32) * invf_ref[...]
    cos = jnp.cos(ang)
    sin = jnp.sin(ang)
    lane = lax.broadcasted_iota(jnp.int32, (1, LANES), 1)
    low = lane < QK_ROPE_DIM

    y = kp * jnp.where(low, cos, sin)
    r = y + pltpu.roll(y, QK_ROPE_DIM, axis=1)
    zero = jnp.zeros_like(r)
    kp_ref[...] = jnp.concatenate([jnp.where(low, r, zero), jnp.where(low, zero, r)], axis=1).astype(BF16)

    q = jnp.dot(_rms(c_q, qnw_ref[...]).astype(BF16), wuq_ref[...], preferred_element_type=F32)
    npe = MLA_HEADS * QK_ROPE_DIM
    nn = MLA_HEADS * QK_NOPE_DIM
    qn_ref[...] = q[:, :nn].astype(BF16)
    cos4 = jnp.concatenate([cos] * (npe // LANES), axis=1)
    sin4 = jnp.concatenate([sin] * (npe // LANES), axis=1)
    qp_ref[...] = (q[:, nn:nn + npe] * cos4 + q[:, nn + npe:] * sin4).astype(BF16)

    kv = jnp.dot(_rms(c_kv, kvnw_ref[...]).astype(BF16), wukv_ref[...], preferred_element_type=F32)
    kn_ref[...] = kv[:, :nn].astype(BF16)
    v_ref[...] = kv[:, nn:].astype(BF16)


def _inproj(x2d, pos2d, invf, nw, win, wkp, qnw, wuq, kvnw, wukv):
    r = IN_ROWS
    row = lambda n: pl.BlockSpec((r, n), lambda i: (i, 0))
    full = lambda a: pl.BlockSpec(a.shape, lambda i: (0,) * a.ndim)
    outs = (SSM_WIDTH, MLA_HEADS * QK_NOPE_DIM, MLA_HEADS * QK_ROPE_DIM,
            MLA_HEADS * QK_NOPE_DIM, MLA_HEADS * V_HEAD_DIM, 2 * LANES)
    return pl.pallas_call(
        _inproj_kernel,
        grid=(TOKENS // r,),
        in_specs=[row(D_MODEL), row(1), full(invf), full(nw), full(win), full(wkp), full(qnw), full(wuq),
                  full(kvnw), full(wukv)],
        out_specs=tuple(row(n) for n in outs),
        out_shape=tuple(jax.ShapeDtypeStruct((TOKENS, n), BF16) for n in outs),
        compiler_params=_params("parallel"),
        name="in_proj",
    )(x2d, pos2d, invf, nw, win, wkp, qnw, wuq, kvnw, wukv)


def _gelu_tanh(x):
    c = math.sqrt(2.0 / math.pi)
    return 0.5 * x * (1.0 + jnp.tanh(c * (x + 0.044715 * (x * x * x))))


def _ssm_kernel(u_ref, perm_ref, permt_ref, bblk_ref, cblk_ref, are_ref, aim_ref, d_ref, wglu_ref, bglu_ref,
                onw_ref, out_ref, bu_scr, s_scr, st_re, st_im, yf_scr):
    t = pl.program_id(0)
    nt = pl.num_programs(0) - 1

    @pl.when(t == 0)
    def _():
        st_re[...] = jnp.zeros_like(st_re)
        st_im[...] = jnp.zeros_like(st_im)

    def glu_logits():
        return jnp.dot(yf_scr[...].astype(BF16), wglu_ref[...], preferred_element_type=F32) + bglu_ref[...]

    def finish(z):
        yf = yf_scr[...]
        out_tm = _rms(yf * _sigmoid(z), onw_ref[...]).astype(BF16)
        out_bm = jnp.dot(permt_ref[...], out_tm, preferred_element_type=F32)
        out_ref[...] = out_bm.astype(BF16).reshape(BATCH, SSM_TIME, SSM_WIDTH)

    def scan_tile(z_prev):
        u_bm = u_ref[...].reshape(SSM_TIME * BATCH, SSM_WIDTH)
        u = jnp.dot(perm_ref[...], u_bm, preferred_element_type=F32).astype(BF16)
        z = None if z_prev is None else z_prev()
        blocks = [slice(g * SSM_ULANES, (g + 1) * SSM_ULANES) for g in range(SSM_NGB)]
        for g, cols in enumerate(blocks):
            bu_scr[g] = jnp.dot(u[:, cols], bblk_ref[g], preferred_element_type=F32)

        ys = []
        for g, cols in enumerate(blocks):
            a_re = jnp.broadcast_to(are_ref[g], (BATCH, SSM_SLANES))
            a_im = jnp.broadcast_to(aim_ref[g], (BATCH, SSM_SLANES))
            s_re = st_re[g]
            s_im = st_im[g]
            for k0 in range(0, SSM_TIME, 2):
                res, ims = [], []
                for k in range(2):
                    rows = slice((k0 + k) * BATCH, (k0 + k + 1) * BATCH)
                    n_re = a_re * s_re - a_im * s_im + bu_scr[g, rows, :SSM_SLANES]
                    n_im = a_re * s_im + a_im * s_re + bu_scr[g, rows, SSM_SLANES:]
                    s_re, s_im = n_re, n_im
                    res.append(n_re)
                    ims.append(n_im)
                pair = slice(k0 * BATCH, (k0 + 2) * BATCH)
                s_scr[g, pair, :SSM_SLANES] = jnp.concatenate(res, axis=0).astype(BF16)
                s_scr[g, pair, SSM_SLANES:] = jnp.concatenate(ims, axis=0).astype(BF16)
            st_re[g] = s_re
            st_im[g] = s_im
            y = lax.dot_general(s_scr[g], cblk_ref[g], (((1,), (1,)), ((), ())), preferred_element_type=F32)
            ys.append(_gelu_tanh(y + d_ref[:, cols] * u[:, cols].astype(F32)))
        if z is not None:
            finish(z)
        yf_scr[...] = jnp.concatenate(ys, axis=1)

    @pl.when(t == 0)
    def _():
        scan_tile(None)

    @pl.when(jnp.logical_and(t > 0, t < nt))
    def _():
        scan_tile(glu_logits)

    @pl.when(t == nt)
    def _():
        finish(glu_logits())


def _ssm(u, perm, permt, bblk, cblk, are, aim, dskip, wglu, bglu, onw):
    rows = SSM_TIME * BATCH
    nt = SEQ // SSM_TIME
    full = lambda a: pl.BlockSpec(a.shape, lambda t: (0,) * a.ndim)
    blk = lambda f: pl.BlockSpec((BATCH, SSM_TIME, SSM_WIDTH), f)
    return pl.pallas_call(
        _ssm_kernel,
        grid=(nt + 1,),
        in_specs=[blk(lambda t: (0, jnp.minimum(t, nt - 1), 0)), full(perm), full(permt), full(bblk), full(cblk),
                  full(are), full(aim), full(dskip), full(wglu), full(bglu), full(onw)],
        out_specs=blk(lambda t: (0, jnp.maximum(t - 1, 0), 0)),
        out_shape=jax.ShapeDtypeStruct((BATCH, SEQ, SSM_WIDTH), BF16),
        scratch_shapes=[pltpu.VMEM((SSM_NGB, rows, 2 * SSM_SLANES), F32),
                        pltpu.VMEM((SSM_NGB, rows, 2 * SSM_SLANES), BF16),
                        pltpu.VMEM((SSM_NGB, BATCH, SSM_SLANES), F32),
                        pltpu.VMEM((SSM_NGB, BATCH, SSM_SLANES), F32),
                        pltpu.VMEM((rows, SSM_WIDTH), F32)],
        compiler_params=_params("arbitrary"),
        name="ssm",
    )(u, perm, permt, bblk, cblk, are, aim, dskip, wglu, bglu, onw)


def _attn_kernel(qn_ref, qp_ref, kn_ref, kp_ref, v_ref, o_ref):
    c = (QK_NOPE_DIM + QK_ROPE_DIM) ** -0.5 * math.log2(math.e)
    neg = float(jnp.finfo(jnp.float32).min)
    heads = range(2)
    hl = [slice(hd * LANES, (hd + 1) * LANES) for hd in heads]
    k = [jnp.concatenate([kn_ref[:, hl[hd]], kp_ref[:, hl[hd]]], axis=1) for hd in heads]
    vt = [v_ref[:, hl[hd]].astype(F32).T.astype(BF16) for hd in heads]
    key = lax.broadcasted_iota(jnp.int32, (ATT_Q, ATT_Q), 0)
    qry = lax.broadcasted_iota(jnp.int32, (ATT_Q, ATT_Q), 1)
    causal = key <= qry

    def scores(hd, qi):
        q0 = qi * ATT_Q
        q = jnp.concatenate([qn_ref[q0:q0 + ATT_Q, hl[hd]], qp_ref[q0:q0 + ATT_Q, :]], axis=1)
        return lax.dot_general(k[hd][:q0 + ATT_Q], q, (((1,), (1,)), ((), ())), preferred_element_type=F32)

    work = [(hd, qi) for qi in range(SEQ // ATT_Q) for hd in heads]
    st_next = scores(*work[0])
    for n, (hd, qi) in enumerate(work):
        q0 = qi * ATT_Q
        kend = q0 + ATT_Q
        st = st_next
        if n + 1 < len(work):
            st_next = scores(*work[n + 1])
        st_diag = jnp.where(causal, st[q0:], neg)
        m = jnp.max(st_diag, axis=0, keepdims=True)
        if qi:
            st_main = st[:q0]
            m = jnp.maximum(m, jnp.max(st_main, axis=0, keepdims=True))
            pt = jnp.concatenate([jnp.exp2((st_main - m) * c), jnp.exp2((st_diag - m) * c)], axis=0)
        else:
            pt = jnp.exp2((st_diag - m) * c)
        denom = jnp.sum(pt, axis=0, keepdims=True)
        ot = jnp.dot(vt[hd][:, :kend], pt.astype(BF16), preferred_element_type=F32)
        o_ref[q0:kend, hl[hd]] = (ot / denom).T.astype(BF16)


def _attn(qn, qp, kn, kp2, v):
    pair = lambda f: pl.BlockSpec((SEQ, 2 * LANES), f)
    return pl.pallas_call(
        _attn_kernel,
        grid=(BATCH, MLA_HEADS // 2),
        in_specs=[pair(lambda b, h: (b, h)), pl.BlockSpec((SEQ, LANES), lambda b, h: (b, h)), pair(lambda b, h: (b, h)),
                  pair(lambda b, h: (b, 0)), pair(lambda b, h: (b, h))],
        out_specs=pair(lambda b, h: (b, h)),
        out_shape=jax.ShapeDtypeStruct((TOKENS, MLA_WIDTH), BF16),
        compiler_params=_params("parallel", "parallel"),
        name="mla_attn",
    )(qn, qp, kn, kp2, v)


def _outproj_kernel(ys_ref, o_ref, x_ref, monw_ref, wtop_ref, wbot_ref, fnw_ref, h_ref, hn_ref):
    on = _rms(o_ref[...].astype(F32), monw_ref[...]).astype(BF16)
    acc = jnp.dot(ys_ref[...], wtop_ref[...], preferred_element_type=F32)
    acc = acc + jnp.dot(on, wbot_ref[...], preferred_element_type=F32)
    h = x_ref[...] + acc
    h_ref[...] = h
    hn_ref[...] = _rms(h, fnw_ref[...]).astype(BF16)


def _outproj(ys, o, x2d, monw, wout, fnw):
    r = OUT_ROWS
    row = lambda n: pl.BlockSpec((r, n), lambda i: (i, 0))
    full = lambda a: pl.BlockSpec(a.shape, lambda i: (0,) * a.ndim)
    return pl.pallas_call(
        _outproj_kernel,
        grid=(TOKENS // r,),
        in_specs=[row(SSM_WIDTH), row(MLA_WIDTH), row(D_MODEL), full(monw),
                  pl.BlockSpec((SSM_WIDTH, D_MODEL), lambda i: (0, 0)),
                  pl.BlockSpec((MLA_WIDTH, D_MODEL), lambda i: (SSM_WIDTH // MLA_WIDTH, 0)), full(fnw)],
        out_specs=(row(D_MODEL), row(D_MODEL)),
        out_shape=(jax.ShapeDtypeStruct((TOKENS, D_MODEL), F32),
                   jax.ShapeDtypeStruct((TOKENS, D_MODEL), BF16)),
        compiler_params=_params("parallel"),
        name="out_proj",
    )(ys, o, x2d, monw, wout, wout, fnw)


def _ffn_kernel(hn_ref, wg_ref, wv_ref, cwg_ref, cwv_ref, cbg_ref, cbv_ref, wd_ref, h_hbm, fnw_ref,
                out_ref, tail_g, tail_v, h_sem):
    i = pl.program_id(0)
    j = pl.program_id(1)

    @pl.when(jnp.logical_and(i == 0, j == 0))
    def _():
        tail_g[...] = jnp.zeros_like(tail_g)
        tail_v[...] = jnp.zeros_like(tail_v)

    def residual_copy():
        rows = pl.ds(pl.multiple_of(i * FFN_ROWS, FFN_ROWS), FFN_ROWS)
        return pltpu.make_async_copy(h_hbm.at[rows, :], out_ref, h_sem)

    def body(first):
        if first:
            residual_copy().start()
        seq_start = (i % (SEQ // FFN_ROWS)) == 0
        x = hn_ref[...]

        def up_conv(w_ref, tail, cw_ref, cb_ref):
            a = jnp.dot(x, w_ref[...], preferred_element_type=F32)
            prev = jnp.where(seq_start, jnp.zeros((CONV_TAIL, FFN_COLS), F32), tail[j])
            tail[j] = a[FFN_ROWS - CONV_TAIL:]
            head = jnp.concatenate([prev, a[:CONV_TAIL]], axis=0)

            def shifted(k):
                return jnp.concatenate([pltpu.roll(head, k, axis=0)[CONV_TAIL:],
                                        pltpu.roll(a, k, axis=0)[CONV_TAIL:]], axis=0)

            cw = cw_ref[...]
            return (cw[0:1, :] * shifted(2) + cw[1:2, :] * shifted(1) + cw[2:3, :] * a) + cb_ref[...]

        g = up_conv(wg_ref, tail_g, cwg_ref, cbg_ref)
        v = up_conv(wv_ref, tail_v, cwv_ref, cbv_ref)
        act = ((g * _sigmoid(g)) * v).astype(BF16)
        if first:
            residual_copy().wait()
        out_ref[...] += jnp.dot(act, wd_ref[...], preferred_element_type=F32)

    pl.when(j == 0)(functools.partial(body, True))
    pl.when(j > 0)(functools.partial(body, False))

    @pl.when(j == pl.num_programs(1) - 1)
    def _():
        out_ref[...] = _rms(out_ref[...], fnw_ref[...])


def _ffn(hn, w_up, conv_w, conv_b, w_down, h, fnw):
    r, c = FFN_ROWS, FFN_COLS
    nj = D_FF // c
    return pl.pallas_call(
        _ffn_kernel,
        grid=(TOKENS // r, nj),
        in_specs=[pl.BlockSpec((r, D_MODEL), lambda i, j: (i, 0)),
                  pl.BlockSpec((D_MODEL, c), lambda i, j: (0, j)),
                  pl.BlockSpec((D_MODEL, c), lambda i, j: (0, nj + j)),
                  pl.BlockSpec((3, c), lambda i, j: (0, j)),
                  pl.BlockSpec((3, c), lambda i, j: (0, nj + j)),
                  pl.BlockSpec((1, c), lambda i, j: (0, j)),
                  pl.BlockSpec((1, c), lambda i, j: (0, nj + j)),
                  pl.BlockSpec((c, D_MODEL), lambda i, j: (j, 0)),
                  pl.BlockSpec(memory_space=pl.ANY),
                  pl.BlockSpec((1, D_MODEL), lambda i, j: (0, 0))],
        out_specs=pl.BlockSpec((r, D_MODEL), lambda i, j: (i, 0)),
        out_shape=jax.ShapeDtypeStruct((TOKENS, D_MODEL), F32),
        scratch_shapes=[pltpu.VMEM((nj, CONV_TAIL, c), F32), pltpu.VMEM((nj, CONV_TAIL, c), F32),
                        pltpu.SemaphoreType.DMA(())],
        compiler_params=_params("arbitrary", "arbitrary"),
        name="conv_ffn",
    )(hn, w_up, w_up, conv_w, conv_w, conv_b, conv_b, w_down, h, fnw)


def _rot_cols(w):
    half = QK_ROPE_DIM // 2
    return jnp.concatenate([-w[..., half:], w[..., :half]], axis=-1)


def _block_diag(w):
    q = w.reshape(SSM_NGB, SSM_GB, SSM_GROUP, SSM_STATE).transpose(0, 2, 1, 3)
    q = q.reshape(SSM_NGB, SSM_GROUP, SSM_SLANES)
    mask = np.arange(SSM_ULANES)[:, None] // SSM_GROUP == np.arange(SSM_SLANES)[None, :] // SSM_STATE
    return jnp.where(jnp.asarray(mask), jnp.tile(q, (1, SSM_GB, 1)), 0.0)


def kernel(x, positions, attn_norm_w, w_in, ssm_lambda_re, ssm_lambda_im, ssm_log_dt, ssm_b_re, ssm_b_im,
           ssm_c_re, ssm_c_im, ssm_d, ssm_w_glu, ssm_b_glu, mla_q_norm_w, mla_w_uq, mla_kv_norm_w, mla_w_ukv,
           ssm_out_norm_w, mla_out_norm_w, w_out, ffn_norm_w, ffn_w_up, ffn_conv_w, ffn_conv_b, ffn_w_down,
           final_norm_w):
    l = 0
    x2d = x.reshape(TOKENS, D_MODEL)
    pos2d = positions.reshape(TOKENS, 1)
    row = lambda a: a.reshape(1, -1)

    kpe0 = SSM_WIDTH + Q_LORA_RANK + KV_LORA_RANK
    w_kpe = w_in[l][:, kpe0:]
    win = w_in[l][:, :kpe0].astype(BF16)
    wkp = jnp.concatenate([w_kpe, _rot_cols(w_kpe)], axis=1).astype(BF16)
    wq = mla_w_uq[l].reshape(Q_LORA_RANK, MLA_HEADS, QK_NOPE_DIM + QK_ROPE_DIM)
    wq_pe = wq[:, :, QK_NOPE_DIM:]
    wuq = jnp.concatenate([wq[:, :, :QK_NOPE_DIM].reshape(Q_LORA_RANK, -1),
                           wq_pe.reshape(Q_LORA_RANK, -1),
                           _rot_cols(wq_pe).reshape(Q_LORA_RANK, -1)], axis=1).astype(BF16)
    wkv = mla_w_ukv[l].reshape(KV_LORA_RANK, MLA_HEADS, QK_NOPE_DIM + V_HEAD_DIM)
    wukv = jnp.concatenate([wkv[:, :, :QK_NOPE_DIM].reshape(KV_LORA_RANK, -1),
                            wkv[:, :, QK_NOPE_DIM:].reshape(KV_LORA_RANK, -1)], axis=1).astype(BF16)
    invf = np.asarray(ROPE_THETA ** (-np.arange(0, QK_ROPE_DIM, 2, dtype=np.float64) / QK_ROPE_DIM), np.float32)
    invf = jnp.asarray(np.tile(invf, LANES // (QK_ROPE_DIM // 2)).reshape(1, LANES))

    u, qn, qp, kn, v, kp2 = _inproj(x2d, pos2d, invf, row(attn_norm_w[l]), win, wkp, row(mla_q_norm_w[l]), wuq,
                                    row(mla_kv_norm_w[l]), wukv)

    abar_re, abar_im, bbar_re, bbar_im = _zoh(ssm_lambda_re[l], ssm_lambda_im[l], ssm_log_dt[l],
                                              ssm_b_re[l].transpose(0, 2, 1), ssm_b_im[l].transpose(0, 2, 1))
    bblk = jnp.concatenate([_block_diag(bbar_re), _block_diag(bbar_im)], axis=2).astype(BF16)
    cblk = jnp.concatenate([_block_diag(ssm_c_re[l]), _block_diag(-ssm_c_im[l])], axis=2).astype(BF16)
    are = abar_re.reshape(SSM_NGB, 1, SSM_SLANES)
    aim = abar_im.reshape(SSM_NGB, 1, SSM_SLANES)
    r_tm = np.arange(SSM_TIME * BATCH)
    perm = np.zeros((r_tm.size, r_tm.size), np.float32)
    perm[r_tm, (r_tm % BATCH) * SSM_TIME + r_tm // BATCH] = 1.0
    ys = _ssm(u.reshape(BATCH, SEQ, SSM_WIDTH), jnp.asarray(perm, BF16), jnp.asarray(perm.T, BF16), bblk, cblk,
              are, aim, row(ssm_d[l]), ssm_w_glu[l].astype(BF16), row(ssm_b_glu[l]), row(ssm_out_norm_w[l]))
    ys = ys.reshape(TOKENS, SSM_WIDTH)

    o = _attn(qn, qp, kn, kp2, v)

    h, hn = _outproj(ys, o, x2d, row(mla_out_norm_w[l]), w_out[l].astype(BF16), row(ffn_norm_w[l]))
    out = _ffn(hn, ffn_w_up[l].astype(BF16), ffn_conv_w[l], row(ffn_conv_b[l]), ffn_w_down[l].astype(BF16),
               h, row(final_norm_w))
    return out.reshape(BATCH, SEQ, D_MODEL)
```

```python
import functools
import math

import numpy as np
import jax
import jax.numpy as jnp
from jax import lax
from jax.experimental import pallas as pl
from jax.experimental.pallas import tpu as pltpu

D_MODEL = 2048
BATCH = 8
SEQ = 2048
TOKENS = BATCH * SEQ
SSM_WIDTH = 1024
SSM_GROUP = 16
SSM_GROUPS = 64
SSM_STATE = 64
QK_NOPE_DIM = 128
QK_ROPE_DIM = 64
V_HEAD_DIM = 128
MLA_WIDTH = 1024
MLA_HEADS = 8
Q_LORA_RANK = 512
KV_LORA_RANK = 256
ROPE_THETA = 10000.0
D_FF = 5632
RMS_EPS = 1e-6

F32 = jnp.float32
BF16 = jnp.bfloat16

LANES = 128
VMEM_LIMIT_BYTES = 56 * 1024 * 1024

IN_ROWS = 512
SSM_TIME = 64
SSM_GB = 16
SSM_NGB = SSM_GROUPS // SSM_GB
SSM_ULANES = SSM_GB * SSM_GROUP
SSM_SLANES = SSM_GB * SSM_STATE
ATT_Q = 512
OUT_ROWS = 512
FFN_ROWS = 1024
FFN_COLS = 512
CONV_TAIL = 8


def _params(*sem):
    return pltpu.CompilerParams(dimension_semantics=sem, vmem_limit_bytes=VMEM_LIMIT_BYTES)


def _rms(x, w):
    return x * lax.rsqrt(jnp.mean(x * x, axis=-1, keepdims=True) + RMS_EPS) * w


def _sigmoid(x):
    return 1.0 / (1.0 + jnp.exp(-x))


def _zoh_kernel(lr_ref, li_ref, ldt_ref, bre_ref, bim_ref, are_ref, aim_ref, bbre_ref, bbim_ref):
    lr = lr_ref[...]
    li = li_ref[...]
    dt = jnp.exp(ldt_ref[...])
    mag = jnp.exp(lr * dt)
    abar_re = mag * jnp.cos(li * dt)
    abar_im = mag * jnp.sin(li * dt)
    nr, ni = abar_re - 1.0, abar_im
    den = lr * lr + li * li
    zr = (nr * lr + ni * li) / den
    zi = (ni * lr - nr * li) / den
    are_ref[...] = abar_re
    aim_ref[...] = abar_im
    bre = bre_ref[...]
    bim = bim_ref[...]
    bbre_ref[...] = zr[:, None, :] * bre - zi[:, None, :] * bim
    bbim_ref[...] = zr[:, None, :] * bim + zi[:, None, :] * bre


def _zoh(lam_re, lam_im, log_dt, b_re_ghp, b_im_ghp):
    gp = jax.ShapeDtypeStruct((SSM_GROUPS, SSM_STATE), F32)
    ghp = jax.ShapeDtypeStruct((SSM_GROUPS, SSM_GROUP, SSM_STATE), F32)
    return pl.pallas_call(
        _zoh_kernel, out_shape=(gp, gp, ghp, ghp), name="ssm_zoh",
    )(lam_re, lam_im, log_dt.reshape(SSM_GROUPS, 1), b_re_ghp, b_im_ghp)


def _inproj_kernel(x_ref, pos_ref, invf_ref, nw_ref, win_ref, wkp_ref, qnw_ref, wuq_ref, kvnw_ref, wukv_ref,
                   u_ref, qn_ref, qp_ref, kn_ref, v_ref, kp_ref):
    x = x_ref[...]
    hn = _rms(x, nw_ref[...])
    hb = hn.astype(BF16)
    proj = jnp.dot(hb, win_ref[...], preferred_element_type=F32)
    kp = jnp.dot(hb, wkp_ref[...], preferred_element_type=F32)
    u_ref[...] = proj[:, :SSM_WIDTH].astype(BF16)
    c_q = proj[:, SSM_WIDTH:SSM_WIDTH + Q_LORA_RANK]
    c_kv = proj[:, SSM_WIDTH + Q_LORA_RANK:]

    ang = pos_ref[...].astype(F32) * invf_ref[...]
    cos = jnp.cos(ang)
    sin = jnp.sin(ang)
    lane = lax.broadcasted_iota(jnp.int32, (1, LANES), 1)
    low = lane < QK_ROPE_DIM

    y = kp * jnp.where(low, cos, sin)
    r = y + pltpu.roll(y, QK_ROPE_DIM, axis=1)
    zero = jnp.zeros_like(r)
    kp_ref[...] = jnp.concatenate([jnp.where(low, r, zero), jnp.where(low, zero, r)], axis=1).astype(BF16)

    q = jnp.dot(_rms(c_q, qnw_ref[...]).astype(BF16), wuq_ref[...], preferred_element_type=F32)
    npe = MLA_HEADS * QK_ROPE_DIM
    nn = MLA_HEADS * QK_NOPE_DIM
    qn_ref[...] = q[:, :nn].astype(BF16)
    cos4 = jnp.concatenate([cos] * (npe // LANES), axis=1)
    sin4 = jnp.concatenate([sin] * (npe // LANES), axis=1)
    qp_ref[...] = (q[:, nn:nn + npe] * cos4 + q[:, nn + npe:] * sin4).astype(BF16)

    kv = jnp.dot(_rms(c_kv, kvnw_ref[...]).astype(BF16), wukv_ref[...], preferred_element_type=F32)
    kn_ref[...] = kv[:, :nn].astype(BF16)
    v_ref[...] = kv[:, nn:].astype(BF16)


def _inproj(x2d, pos2d, invf, nw, win, wkp, qnw, wuq, kvnw, wukv):
    r = IN_ROWS
    row = lambda n: pl.BlockSpec((r, n), lambda i: (i, 0))
    full = lambda a: pl.BlockSpec(a.shape, lambda i: (0,) * a.ndim)
    outs = (SSM_WIDTH, MLA_HEADS * QK_NOPE_DIM, MLA_HEADS * QK_ROPE_DIM,
            MLA_HEADS * QK_NOPE_DIM, MLA_HEADS * V_HEAD_DIM, 2 * LANES)
    return pl.pallas_call(
        _inproj_kernel,
        grid=(TOKENS // r,),
        in_specs=[row(D_MODEL), row(1), full(invf), full(nw), full(win), full(wkp), full(qnw), full(wuq),
                  full(kvnw), full(wukv)],
        out_specs=tuple(row(n) for n in outs),
        out_shape=tuple(jax.ShapeDtypeStruct((TOKENS, n), BF16) for n in outs),
        compiler_params=_params("parallel"),
        name="in_proj",
    )(x2d, pos2d, invf, nw, win, wkp, qnw, wuq, kvnw, wukv)


def _gelu_tanh(x):
    c = math.sqrt(2.0 / math.pi)
    return 0.5 * x * (1.0 + jnp.tanh(c * (x + 0.044715 * (x * x * x))))


def _ssm_kernel(u_ref, perm_ref, permt_ref, bblk_ref, cblk_ref, are_ref, aim_ref, d_ref, wglu_ref, bglu_ref,
                onw_ref, out_ref, bu_scr, s_scr, st_re, st_im):
    @pl.when(pl.program_id(0) == 0)
    def _():
        st_re[...] = jnp.zeros_like(st_re)
        st_im[...] = jnp.zeros_like(st_im)

    u_bm = u_ref[...].reshape(SSM_TIME * BATCH, SSM_WIDTH)
    u = jnp.dot(perm_ref[...], u_bm, preferred_element_type=F32).astype(BF16)
    blocks = [slice(g * SSM_ULANES, (g + 1) * SSM_ULANES) for g in range(SSM_NGB)]
    for g, cols in enumerate(blocks):
        bu_scr[g] = jnp.dot(u[:, cols], bblk_ref[g], preferred_element_type=F32)

    ys = []
    for g, cols in enumerate(blocks):
        a_re = jnp.broadcast_to(are_ref[g], (BATCH, SSM_SLANES))
        a_im = jnp.broadcast_to(aim_ref[g], (BATCH, SSM_SLANES))
        s_re = st_re[g]
        s_im = st_im[g]
        for t in range(0, SSM_TIME, 2):
            res, ims = [], []
            for k in range(2):
                rows = slice((t + k) * BATCH, (t + k + 1) * BATCH)
                n_re = a_re * s_re - a_im * s_im + bu_scr[g, rows, :SSM_SLANES]
                n_im = a_re * s_im + a_im * s_re + bu_scr[g, rows, SSM_SLANES:]
                s_re, s_im = n_re, n_im
                res.append(n_re)
                ims.append(n_im)
            pair = slice(t * BATCH, (t + 2) * BATCH)
            s_scr[g, pair, :SSM_SLANES] = jnp.concatenate(res, axis=0).astype(BF16)
            s_scr[g, pair, SSM_SLANES:] = jnp.concatenate(ims, axis=0).astype(BF16)
        st_re[g] = s_re
        st_im[g] = s_im
        y = lax.dot_general(s_scr[g], cblk_ref[g], (((1,), (1,)), ((), ())), preferred_element_type=F32)
        ys.append(_gelu_tanh(y + d_ref[:, cols] * u[:, cols].astype(F32)))

    yf = jnp.concatenate(ys, axis=1)
    z = jnp.dot(yf.astype(BF16), wglu_ref[...], preferred_element_type=F32) + bglu_ref[...]
    out_tm = _rms(yf * _sigmoid(z), onw_ref[...]).astype(BF16)
    out_bm = jnp.dot(permt_ref[...], out_tm, preferred_element_type=F32)
    out_ref[...] = out_bm.astype(BF16).reshape(BATCH, SSM_TIME, SSM_WIDTH)


def _ssm(u, perm, permt, bblk, cblk, are, aim, dskip, wglu, bglu, onw):
    rows = SSM_TIME * BATCH
    full = lambda a: pl.BlockSpec(a.shape, lambda t: (0,) * a.ndim)
    seq_blk = pl.BlockSpec((BATCH, SSM_TIME, SSM_WIDTH), lambda t: (0, t, 0))
    return pl.pallas_call(
        _ssm_kernel,
        grid=(SEQ // SSM_TIME,),
        in_specs=[seq_blk, full(perm), full(permt), full(bblk), full(cblk), full(are), full(aim), full(dskip),
                  full(wglu), full(bglu), full(onw)],
        out_specs=seq_blk,
        out_shape=jax.ShapeDtypeStruct((BATCH, SEQ, SSM_WIDTH), BF16),
        scratch_shapes=[pltpu.VMEM((SSM_NGB, rows, 2 * SSM_SLANES), F32),
                        pltpu.VMEM((SSM_NGB, rows, 2 * SSM_SLANES), BF16),
                        pltpu.VMEM((SSM_NGB, BATCH, SSM_SLANES), F32),
                        pltpu.VMEM((SSM_NGB, BATCH, SSM_SLANES), F32)],
        compiler_params=_params("arbitrary"),
        name="ssm",
    )(u, perm, permt, bblk, cblk, are, aim, dskip, wglu, bglu, onw)


def _attn_kernel(qn_ref, qp_ref, kn_ref, kp_ref, v_ref, o_ref):
    c = (QK_NOPE_DIM + QK_ROPE_DIM) ** -0.5 * math.log2(math.e)
    neg = float(jnp.finfo(jnp.float32).min)
    heads = range(2)
    hl = [slice(hd * LANES, (hd + 1) * LANES) for hd in heads]
    k = [jnp.concatenate([kn_ref[:, hl[hd]], kp_ref[:, hl[hd]]], axis=1) for hd in heads]
    vt = [v_ref[:, hl[hd]].astype(F32).T.astype(BF16) for hd in heads]
    key = lax.broadcasted_iota(jnp.int32, (ATT_Q, ATT_Q), 0)
    qry = lax.broadcasted_iota(jnp.int32, (ATT_Q, ATT_Q), 1)
    causal = key <= qry

    def scores(hd, qi):
        q0 = qi * ATT_Q
        q = jnp.concatenate([qn_ref[q0:q0 + ATT_Q, hl[hd]], qp_ref[q0:q0 + ATT_Q, :]], axis=1)
        return lax.dot_general(k[hd][:q0 + ATT_Q], q, (((1,), (1,)), ((), ())), preferred_element_type=F32)

    work = [(hd, qi) for qi in range(SEQ // ATT_Q) for hd in heads]
    st_next = scores(*work[0])
    for n, (hd, qi) in enumerate(work):
        q0 = qi * ATT_Q
        kend = q0 + ATT_Q
        st = st_next
        if n + 1 < len(work):
            st_next = scores(*work[n + 1])
        st_diag = jnp.where(causal, st[q0:], neg)
        m = jnp.max(st_diag, axis=0, keepdims=True)
        if qi:
            st_main = st[:q0]
            m = jnp.maximum(m, jnp.max(st_main, axis=0, keepdims=True))
            pt = jnp.concatenate([jnp.exp2((st_main - m) * c), jnp.exp2((st_diag - m) * c)], axis=0)
        else:
            pt = jnp.exp2((st_diag - m) * c)
        denom = jnp.sum(pt, axis=0, keepdims=True)
        ot = jnp.dot(vt[hd][:, :kend], pt.astype(BF16), preferred_element_type=F32)
        o_ref[q0:kend, hl[hd]] = (ot / denom).T.astype(BF16)


def _attn(qn, qp, kn, kp2, v):
    pair = lambda f: pl.BlockSpec((SEQ, 2 * LANES), f)
    return pl.pallas_call(
        _attn_kernel,
        grid=(BATCH, MLA_HEADS // 2),
        in_specs=[pair(lambda b, h: (b, h)), pl.BlockSpec((SEQ, LANES), lambda b, h: (b, h)), pair(lambda b, h: (b, h)),
                  pair(lambda b, h: (b, 0)), pair(lambda b, h: (b, h))],
        out_specs=pair(lambda b, h: (b, h)),
        out_shape=jax.ShapeDtypeStruct((TOKENS, MLA_WIDTH), BF16),
        compiler_params=_params("parallel", "parallel"),
        name="mla_attn",
    )(qn, qp, kn, kp2, v)


def _outproj_kernel(ys_ref, o_ref, x_ref, monw_ref, wtop_ref, wbot_ref, fnw_ref, h_ref, hn_ref):
    on = _rms(o_ref[...].astype(F32), monw_ref[...]).astype(BF16)
    acc = jnp.dot(ys_ref[...], wtop_ref[...], preferred_element_type=F32)
    acc = acc + jnp.dot(on, wbot_ref[...], preferred_element_type=F32)
    h = x_ref[...] + acc
    h_ref[...] = h
    hn_ref[...] = _rms(h, fnw_ref[...]).astype(BF16)


def _outproj(ys, o, x2d, monw, wout, fnw):
    r = OUT_ROWS
    row = lambda n: pl.BlockSpec((r, n), lambda i: (i, 0))
    full = lambda a: pl.BlockSpec(a.shape, lambda i: (0,) * a.ndim)
    return pl.pallas_call(
        _outproj_kernel,
        grid=(TOKENS // r,),
        in_specs=[row(SSM_WIDTH), row(MLA_WIDTH), row(D_MODEL), full(monw),
                  pl.BlockSpec((SSM_WIDTH, D_MODEL), lambda i: (0, 0)),
                  pl.BlockSpec((MLA_WIDTH, D_MODEL), lambda i: (SSM_WIDTH // MLA_WIDTH, 0)), full(fnw)],
        out_specs=(row(D_MODEL), row(D_MODEL)),
        out_shape=(jax.ShapeDtypeStruct((TOKENS, D_MODEL), F32),
                   jax.ShapeDtypeStruct((TOKENS, D_MODEL), BF16)),
        compiler_params=_params("parallel"),
        name="out_proj",
    )(ys, o, x2d, monw, wout, wout, fnw)


def _ffn_kernel(hn_ref, wg_ref, wv_ref, cwg_ref, cwv_ref, cbg_ref, cbv_ref, wd_ref, h_hbm, fnw_ref,
                out_ref, tail_g, tail_v, h_sem):
    i = pl.program_id(0)
    j = pl.program_id(1)

    @pl.when(jnp.logical_and(i == 0, j == 0))
    def _():
        tail_g[...] = jnp.zeros_like(tail_g)
        tail_v[...] = jnp.zeros_like(tail_v)

    def residual_copy():
        rows = pl.ds(pl.multiple_of(i * FFN_ROWS, FFN_ROWS), FFN_ROWS)
        return pltpu.make_async_copy(h_hbm.at[rows, :], out_ref, h_sem)

    def body(first):
        if first:
            residual_copy().start()
        seq_start = (i % (SEQ // FFN_ROWS)) == 0
        x = hn_ref[...]

        def up_conv(w_ref, tail, cw_ref, cb_ref):
            a = jnp.dot(x, w_ref[...], preferred_element_type=F32)
            prev = jnp.where(seq_start, jnp.zeros((CONV_TAIL, FFN_COLS), F32), tail[j])
            tail[j] = a[FFN_ROWS - CONV_TAIL:]
            head = jnp.concatenate([prev, a[:CONV_TAIL]], axis=0)

            def shifted(k):
                return jnp.concatenate([pltpu.roll(head, k, axis=0)[CONV_TAIL:],
                                        pltpu.roll(a, k, axis=0)[CONV_TAIL:]], axis=0)

            cw = cw_ref[...]
            return (cw[0:1, :] * shifted(2) + cw[1:2, :] * shifted(1) + cw[2:3, :] * a) + cb_ref[...]

        g = up_conv(wg_ref, tail_g, cwg_ref, cbg_ref)
        v = up_conv(wv_ref, tail_v, cwv_ref, cbv_ref)
        act = ((g * _sigmoid(g)) * v).astype(BF16)
        if first:
            residual_copy().wait()
        out_ref[...] += jnp.dot(act, wd_ref[...], preferred_element_type=F32)

    pl.when(j == 0)(functools.partial(body, True))
    pl.when(j > 0)(functools.partial(body, False))

    @pl.when(j == pl.num_programs(1) - 1)
    def _():
        out_ref[...] = _rms(out_ref[...], fnw_ref[...])


def _ffn(hn, w_up, conv_w, conv_b, w_down, h, fnw):
    r, c = FFN_ROWS, FFN_COLS
    nj = D_FF // c
    return pl.pallas_call(
        _ffn_kernel,
        grid=(TOKENS // r, nj),
        in_specs=[pl.BlockSpec((r, D_MODEL), lambda i, j: (i, 0)),
                  pl.BlockSpec((D_MODEL, c), lambda i, j: (0, j)),
                  pl.BlockSpec((D_MODEL, c), lambda i, j: (0, nj + j)),
                  pl.BlockSpec((3, c), lambda i, j: (0, j)),
                  pl.BlockSpec((3, c), lambda i, j: (0, nj + j)),
                  pl.BlockSpec((1, c), lambda i, j: (0, j)),
                  pl.BlockSpec((1, c), lambda i, j: (0, nj + j)),
                  pl.BlockSpec((c, D_MODEL), lambda i, j: (j, 0)),
                  pl.BlockSpec(memory_space=pl.ANY),
                  pl.BlockSpec((1, D_MODEL), lambda i, j: (0, 0))],
        out_specs=pl.BlockSpec((r, D_MODEL), lambda i, j: (i, 0)),
        out_shape=jax.ShapeDtypeStruct((TOKENS, D_MODEL), F32),
        scratch_shapes=[pltpu.VMEM((nj, CONV_TAIL, c), F32), pltpu.VMEM((nj, CONV_TAIL, c), F32),
                        pltpu.SemaphoreType.DMA(())],
        compiler_params=_params("arbitrary", "arbitrary"),
        name="conv_ffn",
    )(hn, w_up, w_up, conv_w, conv_w, conv_b, conv_b, w_down, h, fnw)


def _rot_cols(w):
    half = QK_ROPE_DIM // 2
    return jnp.concatenate([-w[..., half:], w[..., :half]], axis=-1)


def _block_diag(w):
    q = w.reshape(SSM_NGB, SSM_GB, SSM_GROUP, SSM_STATE).transpose(0, 2, 1, 3)
    q = q.reshape(SSM_NGB, SSM_GROUP, SSM_SLANES)
    mask = np.arange(SSM_ULANES)[:, None] // SSM_GROUP == np.arange(SSM_SLANES)[None, :] // SSM_STATE
    return jnp.where(jnp.asarray(mask), jnp.tile(q, (1, SSM_GB, 1)), 0.0)


def kernel(x, positions, attn_norm_w, w_in, ssm_lambda_re, ssm_lambda_im, ssm_log_dt, ssm_b_re, ssm_b_im,
           ssm_c_re, ssm_c_im, ssm_d, ssm_w_glu, ssm_b_glu, mla_q_norm_w, mla_w_uq, mla_kv_norm_w, mla_w_ukv,
           ssm_out_norm_w, mla_out_norm_w, w_out, ffn_norm_w, ffn_w_up, ffn_conv_w, ffn_conv_b, ffn_w_down,
           final_norm_w):
    l = 0
    x2d = x.reshape(TOKENS, D_MODEL)
    pos2d = positions.reshape(TOKENS, 1)
    row = lambda a: a.reshape(1, -1)

    kpe0 = SSM_WIDTH + Q_LORA_RANK + KV_LORA_RANK
    w_kpe = w_in[l][:, kpe0:]
    win = w_in[l][:, :kpe0].astype(BF16)
    wkp = jnp.concatenate([w_kpe, _rot_cols(w_kpe)], axis=1).astype(BF16)
    wq = mla_w_uq[l].reshape(Q_LORA_RANK, MLA_HEADS, QK_NOPE_DIM + QK_ROPE_DIM)
    wq_pe = wq[:, :, QK_NOPE_DIM:]
    wuq = jnp.concatenate([wq[:, :, :QK_NOPE_DIM].reshape(Q_LORA_RANK, -1),
                           wq_pe.reshape(Q_LORA_RANK, -1),
                           _rot_cols(wq_pe).reshape(Q_LORA_RANK, -1)], axis=1).astype(BF16)
    wkv = mla_w_ukv[l].reshape(KV_LORA_RANK, MLA_HEADS, QK_NOPE_DIM + V_HEAD_DIM)
    wukv = jnp.concatenate([wkv[:, :, :QK_NOPE_DIM].reshape(KV_LORA_RANK, -1),
                            wkv[:, :, QK_NOPE_DIM:].reshape(KV_LORA_RANK, -1)], axis=1).astype(BF16)
    invf = np.asarray(ROPE_THETA ** (-np.arange(0, QK_ROPE_DIM, 2, dtype=np.float64) / QK_ROPE_DIM), np.float32)
    invf = jnp.asarray(np.tile(invf, LANES // (QK_ROPE_DIM // 2)).reshape(1, LANES))

    u, qn, qp, kn, v, kp2 = _inproj(x2d, pos2d, invf, row(attn_norm_w[l]), win, wkp, row(mla_q_norm_w[l]), wuq,
                                    row(mla_kv_norm_w[l]), wukv)

    abar_re, abar_im, bbar_re, bbar_im = _zoh(ssm_lambda_re[l], ssm_lambda_im[l], ssm_log_dt[l],
                                              ssm_b_re[l].transpose(0, 2, 1), ssm_b_im[l].transpose(0, 2, 1))
    bblk = jnp.concatenate([_block_diag(bbar_re), _block_diag(bbar_im)], axis=2).astype(BF16)
    cblk = jnp.concatenate([_block_diag(ssm_c_re[l]), _block_diag(-ssm_c_im[l])], axis=2).astype(BF16)
    are = abar_re.reshape(SSM_NGB, 1, SSM_SLANES)
    aim = abar_im.reshape(SSM_NGB, 1, SSM_SLANES)
    r_tm = np.arange(SSM_TIME * BATCH)
    perm = np.zeros((r_tm.size, r_tm.size), np.float32)
    perm[r_tm, (r_tm % BATCH) * SSM_TIME + r_tm // BATCH] = 1.0
    ys = _ssm(u.reshape(BATCH, SEQ, SSM_WIDTH), jnp.asarray(perm, BF16), jnp.asarray(perm.T, BF16), bblk, cblk,
              are, aim, row(ssm_d[l]), ssm_w_glu[l].astype(BF16), row(ssm_b_glu[l]), row(ssm_out_norm_w[l]))
    ys = ys.reshape(TOKENS, SSM_WIDTH)

    o = _attn(qn, qp, kn, kp2, v)

    h, hn = _outproj(ys, o, x2d, row(mla_out_norm_w[l]), w_out[l].astype(BF16), row(ffn_norm_w[l]))
    out = _ffn(hn, ffn_w_up[l].astype(BF16), ffn_conv_w[l], row(ffn_conv_b[l]), ffn_w_down[l].astype(BF16),
               h, row(final_norm_w))
    return out.reshape(BATCH, SEQ, D_MODEL)
```

```python
import functools
import math

import numpy as np
import jax
import jax.numpy as jnp
from jax import lax
from jax.experimental import pallas as pl
from jax.experimental.pallas import tpu as pltpu

D_MODEL = 2048
BATCH = 8
SEQ = 2048
TOKENS = BATCH * SEQ
SSM_WIDTH = 1024
SSM_GROUP = 16
SSM_GROUPS = 64
SSM_STATE = 64
QK_NOPE_DIM = 128
QK_ROPE_DIM = 64
V_HEAD_DIM = 128
MLA_WIDTH = 1024
MLA_HEADS = 8
Q_LORA_RANK = 512
KV_LORA_RANK = 256
ROPE_THETA = 10000.0
D_FF = 5632
RMS_EPS = 1e-6

F32 = jnp.float32
BF16 = jnp.bfloat16

LANES = 128
VMEM_LIMIT_BYTES = 56 * 1024 * 1024

IN_ROWS = 512
SSM_TIME = 64
SSM_GB = 16
SSM_NGB = SSM_GROUPS // SSM_GB
SSM_ULANES = SSM_GB * SSM_GROUP
SSM_SLANES = SSM_GB * SSM_STATE
ATT_Q = 512
OUT_ROWS = 512
FFN_ROWS = 1024
FFN_COLS = 512
CONV_TAIL = 8


def _params(*sem):
    return pltpu.CompilerParams(dimension_semantics=sem, vmem_limit_bytes=VMEM_LIMIT_BYTES)


def _rms(x, w):
    return x * lax.rsqrt(jnp.mean(x * x, axis=-1, keepdims=True) + RMS_EPS) * w


def _sigmoid(x):
    return 1.0 / (1.0 + jnp.exp(-x))


def _zoh_kernel(lr_ref, li_ref, ldt_ref, bre_ref, bim_ref, are_ref, aim_ref, bbre_ref, bbim_ref):
    lr = lr_ref[...]
    li = li_ref[...]
    dt = jnp.exp(ldt_ref[...])
    mag = jnp.exp(lr * dt)
    abar_re = mag * jnp.cos(li * dt)
    abar_im = mag * jnp.sin(li * dt)
    nr, ni = abar_re - 1.0, abar_im
    den = lr * lr + li * li
    zr = (nr * lr + ni * li) / den
    zi = (ni * lr - nr * li) / den
    are_ref[...] = abar_re
    aim_ref[...] = abar_im
    bre = bre_ref[...]
    bim = bim_ref[...]
    bbre_ref[...] = zr[:, None, :] * bre - zi[:, None, :] * bim
    bbim_ref[...] = zr[:, None, :] * bim + zi[:, None, :] * bre


def _zoh(lam_re, lam_im, log_dt, b_re_ghp, b_im_ghp):
    gp = jax.ShapeDtypeStruct((SSM_GROUPS, SSM_STATE), F32)
    ghp = jax.ShapeDtypeStruct((SSM_GROUPS, SSM_GROUP, SSM_STATE), F32)
    return pl.pallas_call(
        _zoh_kernel, out_shape=(gp, gp, ghp, ghp), name="ssm_zoh",
    )(lam_re, lam_im, log_dt.reshape(SSM_GROUPS, 1), b_re_ghp, b_im_ghp)


def _inproj_kernel(x_ref, pos_ref, invf_ref, nw_ref, win_ref, wkp_ref, qnw_ref, wuq_ref, kvnw_ref, wukv_ref,
                   u_ref, qn_ref, qp_ref, kn_ref, v_ref, kp_ref):
    x = x_ref[...]
    hn = _rms(x, nw_ref[...])
    hb = hn.astype(BF16)
    proj = jnp.dot(hb, win_ref[...], preferred_element_type=F32)
    kp = jnp.dot(hb, wkp_ref[...], preferred_element_type=F32)
    u_ref[...] = proj[:, :SSM_WIDTH].astype(BF16)
    c_q = proj[:, SSM_WIDTH:SSM_WIDTH + Q_LORA_RANK]
    c_kv = proj[:, SSM_WIDTH + Q_LORA_RANK:]

    ang = pos_ref[...].astype(F32) * invf_ref[...]
    cos = jnp.cos(ang)
    sin = jnp.sin(ang)
    lane = lax.broadcasted_iota(jnp.int32, (1, LANES), 1)
    low = lane < QK_ROPE_DIM

    y = kp * jnp.where(low, cos, sin)
    r = y + pltpu.roll(y, QK_ROPE_DIM, axis=1)
    zero = jnp.zeros_like(r)
    kp_ref[...] = jnp.concatenate([jnp.where(low, r, zero), jnp.where(low, zero, r)], axis=1).astype(BF16)

    q = jnp.dot(_rms(c_q, qnw_ref[...]).astype(BF16), wuq_ref[...], preferred_element_type=F32)
    npe = MLA_HEADS * QK_ROPE_DIM
    nn = MLA_HEADS * QK_NOPE_DIM
    qn_ref[...] = q[:, :nn].astype(BF16)
    cos4 = jnp.concatenate([cos] * (npe // LANES), axis=1)
    sin4 = jnp.concatenate([sin] * (npe // LANES), axis=1)
    qp_ref[...] = (q[:, nn:nn + npe] * cos4 + q[:, nn + npe:] * sin4).astype(BF16)

    kv = jnp.dot(_rms(c_kv, kvnw_ref[...]).astype(BF16), wukv_ref[...], preferred_element_type=F32)
    kn_ref[...] = kv[:, :nn].astype(BF16)
    v_ref[...] = kv[:, nn:].astype(BF16)


def _inproj(x2d, pos2d, invf, nw, win, wkp, qnw, wuq, kvnw, wukv):
    r = IN_ROWS
    row = lambda n: pl.BlockSpec((r, n), lambda i: (i, 0))
    full = lambda a: pl.BlockSpec(a.shape, lambda i: (0,) * a.ndim)
    outs = (SSM_WIDTH, MLA_HEADS * QK_NOPE_DIM, MLA_HEADS * QK_ROPE_DIM,
            MLA_HEADS * QK_NOPE_DIM, MLA_HEADS * V_HEAD_DIM, 2 * LANES)
    return pl.pallas_call(
        _inproj_kernel,
        grid=(TOKENS // r,),
        in_specs=[row(D_MODEL), row(1), full(invf), full(nw), full(win), full(wkp), full(qnw), full(wuq),
                  full(kvnw), full(wukv)],
        out_specs=tuple(row(n) for n in outs),
        out_shape=tuple(jax.ShapeDtypeStruct((TOKENS, n), BF16) for n in outs),
        compiler_params=_params("parallel"),
        name="in_proj",
    )(x2d, pos2d, invf, nw, win, wkp, qnw, wuq, kvnw, wukv)


def _gelu_tanh(x):
    c = math.sqrt(2.0 / math.pi)
    return 0.5 * x * (1.0 + jnp.tanh(c * (x + 0.044715 * (x * x * x))))


def _ssm_kernel(u_ref, perm_ref, permt_ref, bblk_ref, cblk_ref, are_ref, aim_ref, d_ref, wglu_ref, bglu_ref,
                onw_ref, out_ref, bu_scr, s_scr, st_re, st_im):
    @pl.when(pl.program_id(0) == 0)
    def _():
        st_re[...] = jnp.zeros_like(st_re)
        st_im[...] = jnp.zeros_like(st_im)

    u_bm = u_ref[...].reshape(SSM_TIME * BATCH, SSM_WIDTH)
    u = jnp.dot(perm_ref[...], u_bm, preferred_element_type=F32).astype(BF16)
    blocks = [slice(g * SSM_ULANES, (g + 1) * SSM_ULANES) for g in range(SSM_NGB)]
    for g, cols in enumerate(blocks):
        bu_scr[g] = jnp.dot(u[:, cols], bblk_ref[g], preferred_element_type=F32)

    ys = []
    for g, cols in enumerate(blocks):
        a_re = jnp.broadcast_to(are_ref[g], (BATCH, SSM_SLANES))
        a_im = jnp.broadcast_to(aim_ref[g], (BATCH, SSM_SLANES))
        s_re = st_re[g]
        s_im = st_im[g]
        for t in range(0, SSM_TIME, 2):
            res, ims = [], []
            for k in range(2):
                rows = slice((t + k) * BATCH, (t + k + 1) * BATCH)
                n_re = a_re * s_re - a_im * s_im + bu_scr[g, rows, :SSM_SLANES]
                n_im = a_re * s_im + a_im * s_re + bu_scr[g, rows, SSM_SLANES:]
                s_re, s_im = n_re, n_im
                res.append(n_re)
                ims.append(n_im)
            pair = slice(t * BATCH, (t + 2) * BATCH)
            s_scr[g, pair, :SSM_SLANES] = jnp.concatenate(res, axis=0).astype(BF16)
            s_scr[g, pair, SSM_SLANES:] = jnp.concatenate(ims, axis=0).astype(BF16)
        st_re[g] = s_re
        st_im[g] = s_im
        y = lax.dot_general(s_scr[g], cblk_ref[g], (((1,), (1,)), ((), ())), preferred_element_type=F32)
        ys.append(_gelu_tanh(y + d_ref[:, cols] * u[:, cols].astype(F32)))

    yf = jnp.concatenate(ys, axis=1)
    z = jnp.dot(yf.astype(BF16), wglu_ref[...], preferred_element_type=F32) + bglu_ref[...]
    out_tm = _rms(yf * _sigmoid(z), onw_ref[...]).astype(BF16)
    out_bm = jnp.dot(permt_ref[...], out_tm, preferred_element_type=F32)
    out_ref[...] = out_bm.astype(BF16).reshape(BATCH, SSM_TIME, SSM_WIDTH)


def _ssm(u, perm, permt, bblk, cblk, are, aim, dskip, wglu, bglu, onw):
    rows = SSM_TIME * BATCH
    full = lambda a: pl.BlockSpec(a.shape, lambda t: (0,) * a.ndim)
    seq_blk = pl.BlockSpec((BATCH, SSM_TIME, SSM_WIDTH), lambda t: (0, t, 0))
    return pl.pallas_call(
        _ssm_kernel,
        grid=(SEQ // SSM_TIME,),
        in_specs=[seq_blk, full(perm), full(permt), full(bblk), full(cblk), full(are), full(aim), full(dskip),
                  full(wglu), full(bglu), full(onw)],
        out_specs=seq_blk,
        out_shape=jax.ShapeDtypeStruct((BATCH, SEQ, SSM_WIDTH), BF16),
        scratch_shapes=[pltpu.VMEM((SSM_NGB, rows, 2 * SSM_SLANES), F32),
                        pltpu.VMEM((SSM_NGB, rows, 2 * SSM_SLANES), BF16),
                        pltpu.VMEM((SSM_NGB, BATCH, SSM_SLANES), F32),
                        pltpu.VMEM((SSM_NGB, BATCH, SSM_SLANES), F32)],
        compiler_params=_params("arbitrary"),
        name="ssm",
    )(u, perm, permt, bblk, cblk, are, aim, dskip, wglu, bglu, onw)


def _attn_kernel(qn_ref, qp_ref, kn_ref, kp_ref, v_ref, o_ref):
    c = (QK_NOPE_DIM + QK_ROPE_DIM) ** -0.5 * math.log2(math.e)
    neg = float(jnp.finfo(jnp.float32).min)
    heads = range(2)
    hl = [slice(hd * LANES, (hd + 1) * LANES) for hd in heads]
    k = [jnp.concatenate([kn_ref[:, hl[hd]], kp_ref[:, hl[hd]]], axis=1) for hd in heads]
    vt = [v_ref[:, hl[hd]].astype(F32).T.astype(BF16) for hd in heads]
    key = lax.broadcasted_iota(jnp.int32, (ATT_Q, ATT_Q), 0)
    qry = lax.broadcasted_iota(jnp.int32, (ATT_Q, ATT_Q), 1)
    causal = key <= qry

    def scores(hd, qi):
        q0 = qi * ATT_Q
        q = jnp.concatenate([qn_ref[q0:q0 + ATT_Q, hl[hd]], qp_ref[q0:q0 + ATT_Q, :]], axis=1)
        return lax.dot_general(k[hd][:q0 + ATT_Q], q, (((1,), (1,)), ((), ())), preferred_element_type=F32)

    work = [(hd, qi) for qi in range(SEQ // ATT_Q) for hd in heads]
    st_next = scores(*work[0])
    for n, (hd, qi) in enumerate(work):
        q0 = qi * ATT_Q
        kend = q0 + ATT_Q
        st = st_next
        if n + 1 < len(work):
            st_next = scores(*work[n + 1])
        st_diag = jnp.where(causal, st[q0:], neg)
        m = jnp.max(st_diag, axis=0, keepdims=True)
        if qi:
            st_main = st[:q0]
            m = jnp.maximum(m, jnp.max(st_main, axis=0, keepdims=True))
            pt = jnp.concatenate([jnp.exp2((st_main - m) * c), jnp.exp2((st_diag - m) * c)], axis=0)
        else:
            pt = jnp.exp2((st_diag - m) * c)
        denom = jnp.sum(pt, axis=0, keepdims=True)
        ot = jnp.dot(vt[hd][:, :kend], pt.astype(BF16), preferred_element_type=F32)
        o_ref[q0:kend, hl[hd]] = (ot / denom).T.astype(BF16)


def _attn(qn, qp, kn, kp2, v):
    pair = lambda f: pl.BlockSpec((SEQ, 2 * LANES), f)
    return pl.pallas_call(
        _attn_kernel,
        grid=(BATCH, MLA_HEADS // 2),
        in_specs=[pair(lambda b, h: (b, h)), pl.BlockSpec((SEQ, LANES), lambda b, h: (b, h)), pair(lambda b, h: (b, h)),
                  pair(lambda b, h: (b, 0)), pair(lambda b, h: (b, h))],
        out_specs=pair(lambda b, h: (b, h)),
        out_shape=jax.ShapeDtypeStruct((TOKENS, MLA_WIDTH), BF16),
        compiler_params=_params("parallel", "parallel"),
        name="mla_attn",
    )(qn, qp, kn, kp2, v)


def _outproj_kernel(ys_ref, o_ref, x_ref, monw_ref, wtop_ref, wbot_ref, fnw_ref, h_ref, hn_ref):
    on = _rms(o_ref[...].astype(F32), monw_ref[...]).astype(BF16)
    acc = jnp.dot(ys_ref[...], wtop_ref[...], preferred_element_type=F32)
    acc = acc + jnp.dot(on, wbot_ref[...], preferred_element_type=F32)
    h = x_ref[...] + acc
    h_ref[...] = h
    hn_ref[...] = _rms(h, fnw_ref[...]).astype(BF16)


def _outproj(ys, o, x2d, monw, wout, fnw):
    r = OUT_ROWS
    row = lambda n: pl.BlockSpec((r, n), lambda i: (i, 0))
    full = lambda a: pl.BlockSpec(a.shape, lambda i: (0,) * a.ndim)
    return pl.pallas_call(
        _outproj_kernel,
        grid=(TOKENS // r,),
        in_specs=[row(SSM_WIDTH), row(MLA_WIDTH), row(D_MODEL), full(monw),
                  pl.BlockSpec((SSM_WIDTH, D_MODEL), lambda i: (0, 0)),
                  pl.BlockSpec((MLA_WIDTH, D_MODEL), lambda i: (SSM_WIDTH // MLA_WIDTH, 0)), full(fnw)],
        out_specs=(row(D_MODEL), row(D_MODEL)),
        out_shape=(jax.ShapeDtypeStruct((TOKENS, D_MODEL), F32),
                   jax.ShapeDtypeStruct((TOKENS, D_MODEL), BF16)),
        compiler_params=_params("parallel"),
        name="out_proj",
    )(ys, o, x2d, monw, wout, wout, fnw)


def _ffn_kernel(hn_ref, wg_ref, wv_ref, cw_ref, cb_ref, wd_ref, h_hbm, fnw_ref,
                out_ref, tail_g, tail_v, h_sem):
    i = pl.program_id(0)
    j = pl.program_id(1)

    @pl.when(jnp.logical_and(i == 0, j == 0))
    def _():
        tail_g[...] = jnp.zeros_like(tail_g)
        tail_v[...] = jnp.zeros_like(tail_v)

    def residual_copy():
        rows = pl.ds(pl.multiple_of(i * FFN_ROWS, FFN_ROWS), FFN_ROWS)
        return pltpu.make_async_copy(h_hbm.at[rows, :], out_ref, h_sem)

    def body(first):
        if first:
            residual_copy().start()
        seq_start = (i % (SEQ // FFN_ROWS)) == 0
        x = hn_ref[...]

        def up_conv(w_ref, tail, t):
            a = jnp.dot(x, w_ref[...], preferred_element_type=F32)
            prev = jnp.where(seq_start, jnp.zeros((CONV_TAIL, FFN_COLS), F32), tail[j])
            tail[j] = a[FFN_ROWS - CONV_TAIL:]
            head = jnp.concatenate([prev, a[:CONV_TAIL]], axis=0)

            def shifted(k):
                return jnp.concatenate([pltpu.roll(head, k, axis=0)[CONV_TAIL:],
                                        pltpu.roll(a, k, axis=0)[CONV_TAIL:]], axis=0)

            cw = cw_ref[t]
            return (cw[0:1, :] * shifted(2) + cw[1:2, :] * shifted(1) + cw[2:3, :] * a) + cb_ref[t]

        nj = pl.num_programs(1)
        g = up_conv(wg_ref, tail_g, j)
        v = up_conv(wv_ref, tail_v, nj + j)
        act = ((g * _sigmoid(g)) * v).astype(BF16)
        if first:
            residual_copy().wait()
        out_ref[...] += jnp.dot(act, wd_ref[...], preferred_element_type=F32)

    pl.when(j == 0)(functools.partial(body, True))
    pl.when(j > 0)(functools.partial(body, False))

    @pl.when(j == pl.num_programs(1) - 1)
    def _():
        out_ref[...] = _rms(out_ref[...], fnw_ref[...])


def _ffn(hn, w_up, conv_w, conv_b, w_down, h, fnw):
    r, c = FFN_ROWS, FFN_COLS
    nj = D_FF // c
    return pl.pallas_call(
        _ffn_kernel,
        grid=(TOKENS // r, nj),
        in_specs=[pl.BlockSpec((r, D_MODEL), lambda i, j: (i, 0)),
                  pl.BlockSpec((D_MODEL, c), lambda i, j: (0, j)),
                  pl.BlockSpec((D_MODEL, c), lambda i, j: (0, nj + j)),
                  pl.BlockSpec((2 * nj, 3, c), lambda i, j: (0, 0, 0)),
                  pl.BlockSpec((2 * nj, 1, c), lambda i, j: (0, 0, 0)),
                  pl.BlockSpec((c, D_MODEL), lambda i, j: (j, 0)),
                  pl.BlockSpec(memory_space=pl.ANY),
                  pl.BlockSpec((1, D_MODEL), lambda i, j: (0, 0))],
        out_specs=pl.BlockSpec((r, D_MODEL), lambda i, j: (i, 0)),
        out_shape=jax.ShapeDtypeStruct((TOKENS, D_MODEL), F32),
        scratch_shapes=[pltpu.VMEM((nj, CONV_TAIL, c), F32), pltpu.VMEM((nj, CONV_TAIL, c), F32),
                        pltpu.SemaphoreType.DMA(())],
        compiler_params=_params("arbitrary", "arbitrary"),
        name="conv_ffn",
    )(hn, w_up, w_up, conv_w.reshape(3, 2 * nj, c).transpose(1, 0, 2), conv_b.reshape(2 * nj, 1, c), w_down, h, fnw)


def _rot_cols(w):
    half = QK_ROPE_DIM // 2
    return jnp.concatenate([-w[..., half:], w[..., :half]], axis=-1)


def _block_diag(w):
    q = w.reshape(SSM_NGB, SSM_GB, SSM_GROUP, SSM_STATE).transpose(0, 2, 1, 3)
    q = q.reshape(SSM_NGB, SSM_GROUP, SSM_SLANES)
    mask = np.arange(SSM_ULANES)[:, None] // SSM_GROUP == np.arange(SSM_SLANES)[None, :] // SSM_STATE
    return jnp.where(jnp.asarray(mask), jnp.tile(q, (1, SSM_GB, 1)), 0.0)


def kernel(x, positions, attn_norm_w, w_in, ssm_lambda_re, ssm_lambda_im, ssm_log_dt, ssm_b_re, ssm_b_im,
           ssm_c_re, ssm_c_im, ssm_d, ssm_w_glu, ssm_b_glu, mla_q_norm_w, mla_w_uq, mla_kv_norm_w, mla_w_ukv,
           ssm_out_norm_w, mla_out_norm_w, w_out, ffn_norm_w, ffn_w_up, ffn_conv_w, ffn_conv_b, ffn_w_down,
           final_norm_w):
    l = 0
    x2d = x.reshape(TOKENS, D_MODEL)
    pos2d = positions.reshape(TOKENS, 1)
    row = lambda a: a.reshape(1, -1)

    kpe0 = SSM_WIDTH + Q_LORA_RANK + KV_LORA_RANK
    w_kpe = w_in[l][:, kpe0:]
    win = w_in[l][:, :kpe0].astype(BF16)
    wkp = jnp.concatenate([w_kpe, _rot_cols(w_kpe)], axis=1).astype(BF16)
    wq = mla_w_uq[l].reshape(Q_LORA_RANK, MLA_HEADS, QK_NOPE_DIM + QK_ROPE_DIM)
    wq_pe = wq[:, :, QK_NOPE_DIM:]
    wuq = jnp.concatenate([wq[:, :, :QK_NOPE_DIM].reshape(Q_LORA_RANK, -1),
                           wq_pe.reshape(Q_LORA_RANK, -1),
                           _rot_cols(wq_pe).reshape(Q_LORA_RANK, -1)], axis=1).astype(BF16)
    wkv = mla_w_ukv[l].reshape(KV_LORA_RANK, MLA_HEADS, QK_NOPE_DIM + V_HEAD_DIM)
    wukv = jnp.concatenate([wkv[:, :, :QK_NOPE_DIM].reshape(KV_LORA_RANK, -1),
                            wkv[:, :, QK_NOPE_DIM:].reshape(KV_LORA_RANK, -1)], axis=1).astype(BF16)
    invf = np.asarray(ROPE_THETA ** (-np.arange(0, QK_ROPE_DIM, 2, dtype=np.float64) / QK_ROPE_DIM), np.float32)
    invf = jnp.asarray(np.tile(invf, LANES // (QK_ROPE_DIM // 2)).reshape(1, LANES))

    u, qn, qp, kn, v, kp2 = _inproj(x2d, pos2d, invf, row(attn_norm_w[l]), win, wkp, row(mla_q_norm_w[l]), wuq,
                                    row(mla_kv_norm_w[l]), wukv)

    abar_re, abar_im, bbar_re, bbar_im = _zoh(ssm_lambda_re[l], ssm_lambda_im[l], ssm_log_dt[l],
                                              ssm_b_re[l].transpose(0, 2, 1), ssm_b_im[l].transpose(0, 2, 1))
    bblk = jnp.concatenate([_block_diag(bbar_re), _block_diag(bbar_im)], axis=2).astype(BF16)
    cblk = jnp.concatenate([_block_diag(ssm_c_re[l]), _block_diag(-ssm_c_im[l])], axis=2).astype(BF16)
    are = abar_re.reshape(SSM_NGB, 1, SSM_SLANES)
    aim = abar_im.reshape(SSM_NGB, 1, SSM_SLANES)
    r_tm = np.arange(SSM_TIME * BATCH)
    perm = np.zeros((r_tm.size, r_tm.size), np.float32)
    perm[r_tm, (r_tm % BATCH) * SSM_TIME + r_tm // BATCH] = 1.0
    ys = _ssm(u.reshape(BATCH, SEQ, SSM_WIDTH), jnp.asarray(perm, BF16), jnp.asarray(perm.T, BF16), bblk, cblk,
              are, aim, row(ssm_d[l]), ssm_w_glu[l].astype(BF16), row(ssm_b_glu[l]), row(ssm_out_norm_w[l]))
    ys = ys.reshape(TOKENS, SSM_WIDTH)

    o = _attn(qn, qp, kn, kp2, v)

    h, hn = _outproj(ys, o, x2d, row(mla_out_norm_w[l]), w_out[l].astype(BF16), row(ffn_norm_w[l]))
    out = _ffn(hn, ffn_w_up[l].astype(BF16), ffn_conv_w[l], row(ffn_conv_b[l]), ffn_w_down[l].astype(BF16),
               h, row(final_norm_w))
    return out.reshape(BATCH, SEQ, D_MODEL)
```

```python
import functools
import math

import numpy as np
import jax
import jax.numpy as jnp
from jax import lax
from jax.experimental import pallas as pl
from jax.experimental.pallas import tpu as pltpu

D_MODEL = 2048
BATCH = 8
SEQ = 2048
TOKENS = BATCH * SEQ
SSM_WIDTH = 1024
SSM_GROUP = 16
SSM_GROUPS = 64
SSM_STATE = 64
QK_NOPE_DIM = 128
QK_ROPE_DIM = 64
V_HEAD_DIM = 128
MLA_WIDTH = 1024
MLA_HEADS = 8
Q_LORA_RANK = 512
KV_LORA_RANK = 256
ROPE_THETA = 10000.0
D_FF = 5632
RMS_EPS = 1e-6

F32 = jnp.float32
BF16 = jnp.bfloat16

LANES = 128
VMEM_LIMIT_BYTES = 56 * 1024 * 1024

IN_ROWS = 512
SSM_TIME = 64
SSM_GB = 16
SSM_NGB = SSM_GROUPS // SSM_GB
SSM_ULANES = SSM_GB * SSM_GROUP
SSM_SLANES = SSM_GB * SSM_STATE
ATT_Q = 512
OUT_ROWS = 512
FFN_ROWS = 1024
FFN_COLS = 512
CONV_TAIL = 8


def _params(*sem):
    return pltpu.CompilerParams(dimension_semantics=sem, vmem_limit_bytes=VMEM_LIMIT_BYTES)


def _rms(x, w):
    return x * lax.rsqrt(jnp.mean(x * x, axis=-1, keepdims=True) + RMS_EPS) * w


def _sigmoid(x):
    return 1.0 / (1.0 + jnp.exp(-x))


def _zoh_kernel(lr_ref, li_ref, ldt_ref, bre_ref, bim_ref, are_ref, aim_ref, bbre_ref, bbim_ref):
    lr = lr_ref[...]
    li = li_ref[...]
    dt = jnp.exp(ldt_ref[...])
    mag = jnp.exp(lr * dt)
    abar_re = mag * jnp.cos(li * dt)
    abar_im = mag * jnp.sin(li * dt)
    nr, ni = abar_re - 1.0, abar_im
    den = lr * lr + li * li
    zr = (nr * lr + ni * li) / den
    zi = (ni * lr - nr * li) / den
    are_ref[...] = abar_re
    aim_ref[...] = abar_im
    bre = bre_ref[...]
    bim = bim_ref[...]
    bbre_ref[...] = zr[:, None, :] * bre - zi[:, None, :] * bim
    bbim_ref[...] = zr[:, None, :] * bim + zi[:, None, :] * bre


def _zoh(lam_re, lam_im, log_dt, b_re_ghp, b_im_ghp):
    gp = jax.ShapeDtypeStruct((SSM_GROUPS, SSM_STATE), F32)
    ghp = jax.ShapeDtypeStruct((SSM_GROUPS, SSM_GROUP, SSM_STATE), F32)
    return pl.pallas_call(
        _zoh_kernel, out_shape=(gp, gp, ghp, ghp), name="ssm_zoh",
    )(lam_re, lam_im, log_dt.reshape(SSM_GROUPS, 1), b_re_ghp, b_im_ghp)


def _inproj_kernel(x_ref, pos_ref, invf_ref, nw_ref, win_ref, wkp_ref, qnw_ref, wuq_ref, kvnw_ref, wukv_ref,
                   u_ref, qn_ref, qp_ref, kn_ref, v_ref, kp_ref):
    x = x_ref[...]
    hn = _rms(x, nw_ref[...])
    hb = hn.astype(BF16)
    proj = jnp.dot(hb, win_ref[...], preferred_element_type=F32)
    kp = jnp.dot(hb, wkp_ref[...], preferred_element_type=F32)
    u_ref[...] = proj[:, :SSM_WIDTH].astype(BF16)
    c_q = proj[:, SSM_WIDTH:SSM_WIDTH + Q_LORA_RANK]
    c_kv = proj[:, SSM_WIDTH + Q_LORA_RANK:]

    ang = pos_ref[...].astype(F32) * invf_ref[...]
    cos = jnp.cos(ang)
    sin = jnp.sin(ang)
    lane = lax.broadcasted_iota(jnp.int32, (1, LANES), 1)
    low = lane < QK_ROPE_DIM

    y = kp * jnp.where(low, cos, sin)
    r = y + pltpu.roll(y, QK_ROPE_DIM, axis=1)
    zero = jnp.zeros_like(r)
    kp_ref[...] = jnp.concatenate([jnp.where(low, r, zero), jnp.where(low, zero, r)], axis=1).astype(BF16)

    q = jnp.dot(_rms(c_q, qnw_ref[...]).astype(BF16), wuq_ref[...], preferred_element_type=F32)
    npe = MLA_HEADS * QK_ROPE_DIM
    nn = MLA_HEADS * QK_NOPE_DIM
    qn_ref[...] = q[:, :nn].astype(BF16)
    cos4 = jnp.concatenate([cos] * (npe // LANES), axis=1)
    sin4 = jnp.concatenate([sin] * (npe // LANES), axis=1)
    qp_ref[...] = (q[:, nn:nn + npe] * cos4 + q[:, nn + npe:] * sin4).astype(BF16)

    kv = jnp.dot(_rms(c_kv, kvnw_ref[...]).astype(BF16), wukv_ref[...], preferred_element_type=F32)
    kn_ref[...] = kv[:, :nn].astype(BF16)
    v_ref[...] = kv[:, nn:].astype(BF16)


def _inproj(x2d, pos2d, invf, nw, win, wkp, qnw, wuq, kvnw, wukv):
    r = IN_ROWS
    row = lambda n: pl.BlockSpec((r, n), lambda i: (i, 0))
    full = lambda a: pl.BlockSpec(a.shape, lambda i: (0,) * a.ndim)
    outs = (SSM_WIDTH, MLA_HEADS * QK_NOPE_DIM, MLA_HEADS * QK_ROPE_DIM,
            MLA_HEADS * QK_NOPE_DIM, MLA_HEADS * V_HEAD_DIM, 2 * LANES)
    return pl.pallas_call(
        _inproj_kernel,
        grid=(TOKENS // r,),
        in_specs=[row(D_MODEL), row(1), full(invf), full(nw), full(win), full(wkp), full(qnw), full(wuq),
                  full(kvnw), full(wukv)],
        out_specs=tuple(row(n) for n in outs),
        out_shape=tuple(jax.ShapeDtypeStruct((TOKENS, n), BF16) for n in outs),
        compiler_params=_params("parallel"),
        name="in_proj",
    )(x2d, pos2d, invf, nw, win, wkp, qnw, wuq, kvnw, wukv)


def _gelu_tanh(x):
    c = math.sqrt(2.0 / math.pi)
    return 0.5 * x * (1.0 + jnp.tanh(c * (x + 0.044715 * (x * x * x))))


def _ssm_kernel(u_ref, perm_ref, permt_ref, bblk_ref, cblk_ref, are_ref, aim_ref, d_ref, wglu_ref, bglu_ref,
                onw_ref, out_ref, bu_scr, s_scr, st_re, st_im):
    @pl.when(pl.program_id(0) == 0)
    def _():
        st_re[...] = jnp.zeros_like(st_re)
        st_im[...] = jnp.zeros_like(st_im)

    u_bm = u_ref[...].reshape(SSM_TIME * BATCH, SSM_WIDTH)
    u = jnp.dot(perm_ref[...], u_bm, preferred_element_type=F32).astype(BF16)
    blocks = [slice(g * SSM_ULANES, (g + 1) * SSM_ULANES) for g in range(SSM_NGB)]
    for g, cols in enumerate(blocks):
        bu_scr[g] = jnp.dot(u[:, cols], bblk_ref[g], preferred_element_type=F32)

    ys = []
    for g, cols in enumerate(blocks):
        a_re = jnp.broadcast_to(are_ref[g], (BATCH, SSM_SLANES))
        a_im = jnp.broadcast_to(aim_ref[g], (BATCH, SSM_SLANES))
        s_re = st_re[g]
        s_im = st_im[g]
        for t in range(0, SSM_TIME, 2):
            res, ims = [], []
            for k in range(2):
                rows = slice((t + k) * BATCH, (t + k + 1) * BATCH)
                n_re = a_re * s_re - a_im * s_im + bu_scr[g, rows, :SSM_SLANES]
                n_im = a_re * s_im + a_im * s_re + bu_scr[g, rows, SSM_SLANES:]
                s_re, s_im = n_re, n_im
                res.append(n_re)
                ims.append(n_im)
            pair = slice(t * BATCH, (t + 2) * BATCH)
            s_scr[g, pair, :SSM_SLANES] = jnp.concatenate(res, axis=0).astype(BF16)
            s_scr[g, pair, SSM_SLANES:] = jnp.concatenate(ims, axis=0).astype(BF16)
        st_re[g] = s_re
        st_im[g] = s_im
        y = lax.dot_general(s_scr[g], cblk_ref[g], (((1,), (1,)), ((), ())), preferred_element_type=F32)
        ys.append(_gelu_tanh(y + d_ref[:, cols] * u[:, cols].astype(F32)))

    yf = jnp.concatenate(ys, axis=1)
    z = jnp.dot(yf.astype(BF16), wglu_ref[...], preferred_element_type=F32) + bglu_ref[...]
    out_tm = _rms(yf * _sigmoid(z), onw_ref[...]).astype(BF16)
    out_bm = jnp.dot(permt_ref[...], out_tm, preferred_element_type=F32)
    out_ref[...] = out_bm.astype(BF16).reshape(BATCH, SSM_TIME, SSM_WIDTH)


def _ssm(u, perm, permt, bblk, cblk, are, aim, dskip, wglu, bglu, onw):
    rows = SSM_TIME * BATCH
    full = lambda a: pl.BlockSpec(a.shape, lambda t: (0,) * a.ndim)
    seq_blk = pl.BlockSpec((BATCH, SSM_TIME, SSM_WIDTH), lambda t: (0, t, 0))
    return pl.pallas_call(
        _ssm_kernel,
        grid=(SEQ // SSM_TIME,),
        in_specs=[seq_blk, full(perm), full(permt), full(bblk), full(cblk), full(are), full(aim), full(dskip),
                  full(wglu), full(bglu), full(onw)],
        out_specs=seq_blk,
        out_shape=jax.ShapeDtypeStruct((BATCH, SEQ, SSM_WIDTH), BF16),
        scratch_shapes=[pltpu.VMEM((SSM_NGB, rows, 2 * SSM_SLANES), F32),
                        pltpu.VMEM((SSM_NGB, rows, 2 * SSM_SLANES), BF16),
                        pltpu.VMEM((SSM_NGB, BATCH, SSM_SLANES), F32),
                        pltpu.VMEM((SSM_NGB, BATCH, SSM_SLANES), F32)],
        compiler_params=_params("arbitrary"),
        name="ssm",
    )(u, perm, permt, bblk, cblk, are, aim, dskip, wglu, bglu, onw)


def _attn_kernel(qn_ref, qp_ref, kn_ref, kp_ref, v_ref, o_ref):
    c = (QK_NOPE_DIM + QK_ROPE_DIM) ** -0.5 * math.log2(math.e)
    neg = float(jnp.finfo(jnp.float32).min)
    heads = range(4)
    hl = [slice(hd * LANES, (hd + 1) * LANES) for hd in heads]
    k = [jnp.concatenate([kn_ref[:, hl[hd]], kp_ref[:, hl[hd % 2]]], axis=1) for hd in heads]
    vt = [v_ref[:, hl[hd]].astype(F32).T.astype(BF16) for hd in heads]
    key = lax.broadcasted_iota(jnp.int32, (ATT_Q, ATT_Q), 0)
    qry = lax.broadcasted_iota(jnp.int32, (ATT_Q, ATT_Q), 1)
    causal = key <= qry

    def scores(hd, qi):
        q0 = qi * ATT_Q
        q = jnp.concatenate([qn_ref[q0:q0 + ATT_Q, hl[hd]], qp_ref[q0:q0 + ATT_Q, hl[hd // 2]]], axis=1)
        return lax.dot_general(k[hd][:q0 + ATT_Q], q, (((1,), (1,)), ((), ())), preferred_element_type=F32)

    work = [(hd, qi) for qi in range(SEQ // ATT_Q) for hd in heads]
    st_next = scores(*work[0])
    for n, (hd, qi) in enumerate(work):
        q0 = qi * ATT_Q
        kend = q0 + ATT_Q
        st = st_next
        if n + 1 < len(work):
            st_next = scores(*work[n + 1])
        st_diag = jnp.where(causal, st[q0:], neg)
        m = jnp.max(st_diag, axis=0, keepdims=True)
        if qi:
            st_main = st[:q0]
            m = jnp.maximum(m, jnp.max(st_main, axis=0, keepdims=True))
            pt = jnp.concatenate([jnp.exp2((st_main - m) * c), jnp.exp2((st_diag - m) * c)], axis=0)
        else:
            pt = jnp.exp2((st_diag - m) * c)
        denom = jnp.sum(pt, axis=0, keepdims=True)
        ot = jnp.dot(vt[hd][:, :kend], pt.astype(BF16), preferred_element_type=F32)
        o_ref[q0:kend, hl[hd]] = (ot / denom).T.astype(BF16)


def _attn(qn, qp, kn, kp2, v):
    quad = lambda f: pl.BlockSpec((SEQ, 4 * LANES), f)
    pair = lambda f: pl.BlockSpec((SEQ, 2 * LANES), f)
    return pl.pallas_call(
        _attn_kernel,
        grid=(BATCH, MLA_HEADS // 4),
        in_specs=[quad(lambda b, h: (b, h)), pair(lambda b, h: (b, h)), quad(lambda b, h: (b, h)),
                  pair(lambda b, h: (b, 0)), quad(lambda b, h: (b, h))],
        out_specs=quad(lambda b, h: (b, h)),
        out_shape=jax.ShapeDtypeStruct((TOKENS, MLA_WIDTH), BF16),
        compiler_params=_params("parallel", "parallel"),
        name="mla_attn",
    )(qn, qp, kn, kp2, v)


def _outproj_kernel(ys_ref, o_ref, x_ref, monw_ref, wtop_ref, wbot_ref, fnw_ref, h_ref, hn_ref):
    on = _rms(o_ref[...].astype(F32), monw_ref[...]).astype(BF16)
    acc = jnp.dot(ys_ref[...], wtop_ref[...], preferred_element_type=F32)
    acc = acc + jnp.dot(on, wbot_ref[...], preferred_element_type=F32)
    h = x_ref[...] + acc
    h_ref[...] = h
    hn_ref[...] = _rms(h, fnw_ref[...]).astype(BF16)


def _outproj(ys, o, x2d, monw, wout, fnw):
    r = OUT_ROWS
    row = lambda n: pl.BlockSpec((r, n), lambda i: (i, 0))
    full = lambda a: pl.BlockSpec(a.shape, lambda i: (0,) * a.ndim)
    return pl.pallas_call(
        _outproj_kernel,
        grid=(TOKENS // r,),
        in_specs=[row(SSM_WIDTH), row(MLA_WIDTH), row(D_MODEL), full(monw),
                  pl.BlockSpec((SSM_WIDTH, D_MODEL), lambda i: (0, 0)),
                  pl.BlockSpec((MLA_WIDTH, D_MODEL), lambda i: (SSM_WIDTH // MLA_WIDTH, 0)), full(fnw)],
        out_specs=(row(D_MODEL), row(D_MODEL)),
        out_shape=(jax.ShapeDtypeStruct((TOKENS, D_MODEL), F32),
                   jax.ShapeDtypeStruct((TOKENS, D_MODEL), BF16)),
        compiler_params=_params("parallel"),
        name="out_proj",
    )(ys, o, x2d, monw, wout, wout, fnw)


def _ffn_kernel(hn_ref, wg_ref, wv_ref, cwg_ref, cwv_ref, cbg_ref, cbv_ref, wd_ref, h_hbm, fnw_ref,
                out_ref, tail_g, tail_v, h_sem):
    i = pl.program_id(0)
    j = pl.program_id(1)

    @pl.when(jnp.logical_and(i == 0, j == 0))
    def _():
        tail_g[...] = jnp.zeros_like(tail_g)
        tail_v[...] = jnp.zeros_like(tail_v)

    def residual_copy():
        rows = pl.ds(pl.multiple_of(i * FFN_ROWS, FFN_ROWS), FFN_ROWS)
        return pltpu.make_async_copy(h_hbm.at[rows, :], out_ref, h_sem)

    def body(first):
        if first:
            residual_copy().start()
        seq_start = (i % (SEQ // FFN_ROWS)) == 0
        x = hn_ref[...]

        def up_conv(w_ref, tail, cw_ref, cb_ref):
            a = jnp.dot(x, w_ref[...], preferred_element_type=F32)
            prev = jnp.where(seq_start, jnp.zeros((CONV_TAIL, FFN_COLS), F32), tail[j])
            tail[j] = a[FFN_ROWS - CONV_TAIL:]
            head = jnp.concatenate([prev, a[:CONV_TAIL]], axis=0)

            def shifted(k):
                return jnp.concatenate([pltpu.roll(head, k, axis=0)[CONV_TAIL:],
                                        pltpu.roll(a, k, axis=0)[CONV_TAIL:]], axis=0)

            cw = cw_ref[...]
            return (cw[0:1, :] * shifted(2) + cw[1:2, :] * shifted(1) + cw[2:3, :] * a) + cb_ref[...]

        g = up_conv(wg_ref, tail_g, cwg_ref, cbg_ref)
        v = up_conv(wv_ref, tail_v, cwv_ref, cbv_ref)
        act = ((g * _sigmoid(g)) * v).astype(BF16)
        if first:
            residual_copy().wait()
        out_ref[...] += jnp.dot(act, wd_ref[...], preferred_element_type=F32)

    pl.when(j == 0)(functools.partial(body, True))
    pl.when(j > 0)(functools.partial(body, False))

    @pl.when(j == pl.num_programs(1) - 1)
    def _():
        out_ref[...] = _rms(out_ref[...], fnw_ref[...])


def _ffn(hn, w_up, conv_w, conv_b, w_down, h, fnw):
    r, c = FFN_ROWS, FFN_COLS
    nj = D_FF // c
    return pl.pallas_call(
        _ffn_kernel,
        grid=(TOKENS // r, nj),
        in_specs=[pl.BlockSpec((r, D_MODEL), lambda i, j: (i, 0)),
                  pl.BlockSpec((D_MODEL, c), lambda i, j: (0, j)),
                  pl.BlockSpec((D_MODEL, c), lambda i, j: (0, nj + j)),
                  pl.BlockSpec((3, c), lambda i, j: (0, j)),
                  pl.BlockSpec((3, c), lambda i, j: (0, nj + j)),
                  pl.BlockSpec((1, c), lambda i, j: (0, j)),
                  pl.BlockSpec((1, c), lambda i, j: (0, nj + j)),
                  pl.BlockSpec((c, D_MODEL), lambda i, j: (j, 0)),
                  pl.BlockSpec(memory_space=pl.ANY),
                  pl.BlockSpec((1, D_MODEL), lambda i, j: (0, 0))],
        out_specs=pl.BlockSpec((r, D_MODEL), lambda i, j: (i, 0)),
        out_shape=jax.ShapeDtypeStruct((TOKENS, D_MODEL), F32),
        scratch_shapes=[pltpu.VMEM((nj, CONV_TAIL, c), F32), pltpu.VMEM((nj, CONV_TAIL, c), F32),
                        pltpu.SemaphoreType.DMA(())],
        compiler_params=_params("arbitrary", "arbitrary"),
        name="conv_ffn",
    )(hn, w_up, w_up, conv_w, conv_w, conv_b, conv_b, w_down, h, fnw)


def _rot_cols(w):
    half = QK_ROPE_DIM // 2
    return jnp.concatenate([-w[..., half:], w[..., :half]], axis=-1)


def _block_diag(w):
    q = w.reshape(SSM_NGB, SSM_GB, SSM_GROUP, SSM_STATE).transpose(0, 2, 1, 3)
    q = q.reshape(SSM_NGB, SSM_GROUP, SSM_SLANES)
    mask = np.arange(SSM_ULANES)[:, None] // SSM_GROUP == np.arange(SSM_SLANES)[None, :] // SSM_STATE
    return jnp.where(jnp.asarray(mask), jnp.tile(q, (1, SSM_GB, 1)), 0.0)


def kernel(x, positions, attn_norm_w, w_in, ssm_lambda_re, ssm_lambda_im, ssm_log_dt, ssm_b_re, ssm_b_im,
           ssm_c_re, ssm_c_im, ssm_d, ssm_w_glu, ssm_b_glu, mla_q_norm_w, mla_w_uq, mla_kv_norm_w, mla_w_ukv,
           ssm_out_norm_w, mla_out_norm_w, w_out, ffn_norm_w, ffn_w_up, ffn_conv_w, ffn_conv_b, ffn_w_down,
           final_norm_w):
    l = 0
    x2d = x.reshape(TOKENS, D_MODEL)
    pos2d = positions.reshape(TOKENS, 1)
    row = lambda a: a.reshape(1, -1)

    kpe0 = SSM_WIDTH + Q_LORA_RANK + KV_LORA_RANK
    w_kpe = w_in[l][:, kpe0:]
    win = w_in[l][:, :kpe0].astype(BF16)
    wkp = jnp.concatenate([w_kpe, _rot_cols(w_kpe)], axis=1).astype(BF16)
    wq = mla_w_uq[l].reshape(Q_LORA_RANK, MLA_HEADS, QK_NOPE_DIM + QK_ROPE_DIM)
    wq_pe = wq[:, :, QK_NOPE_DIM:]
    wuq = jnp.concatenate([wq[:, :, :QK_NOPE_DIM].reshape(Q_LORA_RANK, -1),
                           wq_pe.reshape(Q_LORA_RANK, -1),
                           _rot_cols(wq_pe).reshape(Q_LORA_RANK, -1)], axis=1).astype(BF16)
    wkv = mla_w_ukv[l].reshape(KV_LORA_RANK, MLA_HEADS, QK_NOPE_DIM + V_HEAD_DIM)
    wukv = jnp.concatenate([wkv[:, :, :QK_NOPE_DIM].reshape(KV_LORA_RANK, -1),
                            wkv[:, :, QK_NOPE_DIM:].reshape(KV_LORA_RANK, -1)], axis=1).astype(BF16)
    invf = np.asarray(ROPE_THETA ** (-np.arange(0, QK_ROPE_DIM, 2, dtype=np.float64) / QK_ROPE_DIM), np.float32)
    invf = jnp.asarray(np.tile(invf, LANES // (QK_ROPE_DIM // 2)).reshape(1, LANES))

    u, qn, qp, kn, v, kp2 = _inproj(x2d, pos2d, invf, row(attn_norm_w[l]), win, wkp, row(mla_q_norm_w[l]), wuq,
                                    row(mla_kv_norm_w[l]), wukv)

    abar_re, abar_im, bbar_re, bbar_im = _zoh(ssm_lambda_re[l], ssm_lambda_im[l], ssm_log_dt[l],
                                              ssm_b_re[l].transpose(0, 2, 1), ssm_b_im[l].transpose(0, 2, 1))
    bblk = jnp.concatenate([_block_diag(bbar_re), _block_diag(bbar_im)], axis=2).astype(BF16)
    cblk = jnp.concatenate([_block_diag(ssm_c_re[l]), _block_diag(-ssm_c_im[l])], axis=2).astype(BF16)
    are = abar_re.reshape(SSM_NGB, 1, SSM_SLANES)
    aim = abar_im.reshape(SSM_NGB, 1, SSM_SLANES)
    r_tm = np.arange(SSM_TIME * BATCH)
    perm = np.zeros((r_tm.size, r_tm.size), np.float32)
    perm[r_tm, (r_tm % BATCH) * SSM_TIME + r_tm // BATCH] = 1.0
    ys = _ssm(u.reshape(BATCH, SEQ, SSM_WIDTH), jnp.asarray(perm, BF16), jnp.asarray(perm.T, BF16), bblk, cblk,
              are, aim, row(ssm_d[l]), ssm_w_glu[l].astype(BF16), row(ssm_b_glu[l]), row(ssm_out_norm_w[l]))
    ys = ys.reshape(TOKENS, SSM_WIDTH)

    o = _attn(qn, qp, kn, kp2, v)

    h, hn = _outproj(ys, o, x2d, row(mla_out_norm_w[l]), w_out[l].astype(BF16), row(ffn_norm_w[l]))
    out = _ffn(hn, ffn_w_up[l].astype(BF16), ffn_conv_w[l], row(ffn_conv_b[l]), ffn_w_down[l].astype(BF16),
               h, row(final_norm_w))
    return out.reshape(BATCH, SEQ, D_MODEL)
```
